```python
import math
import jax, jax.numpy as jnp
from jax import lax
import numpy as np

D_MODEL = 1024
BATCH = 8
SEQ = 2048
DEPTH = 2

HEAD_DIM = 64
FOX_HEADS = 8
DIFF_HEADS = 4
FOX_WIDTH = FOX_HEADS * HEAD_DIM
DIFF_WIDTH = DIFF_HEADS * 2 * HEAD_DIM
MIX_WIDTH = FOX_WIDTH + DIFF_WIDTH
IN_WIDTH = 3 * FOX_WIDTH + FOX_HEADS + 3 * DIFF_WIDTH
D_FF = ((8 * D_MODEL // 3 + 255) // 256) * 256
BLOCK_Q = 128
EPS = 1e-6

kernel_name = "fox_diffattn_hybrid_adaln"


def _rms(x, g):
    xf = x.astype(jnp.float32)
    y = xf * lax.rsqrt(jnp.mean(xf * xf, axis=-1, keepdims=True) + EPS)
    return (y * g.astype(jnp.float32)).astype(x.dtype)


def _to_blocks(t):
    b, h, s = t.shape[:3]
    nb = s // BLOCK_Q
    t = t.reshape((b, h, nb, BLOCK_Q) + t.shape[3:])
    return jnp.moveaxis(t, 2, 0)


def _from_blocks(t):
    nb, b, h, bq, d = t.shape
    return jnp.moveaxis(t, 0, 2).reshape(b, h, nb * bq, d)


def _fox_attention(q, k, v, log_f):
    s_len = q.shape[2]
    scale = 1.0 / math.sqrt(q.shape[-1])
    cum = jnp.cumsum(log_f, axis=-1)
    key_pos = jnp.arange(s_len)
    starts = jnp.arange(s_len // BLOCK_Q) * BLOCK_Q

    def one_block(args):
        qi, ci, st = args
        s = jnp.einsum('bhqd,bhkd->bhqk', qi, k, preferred_element_type=jnp.float32) * scale
        s = s + ci[..., :, None] - cum[..., None, :]
        qpos = st + jnp.arange(BLOCK_Q)
        mask = key_pos[None, :] <= qpos[:, None]
        p = jax.nn.softmax(jnp.where(mask, s, -jnp.inf), axis=-1)
        return jnp.einsum('bhqk,bhkd->bhqd', p.astype(v.dtype), v)

    out = lax.map(one_block, (_to_blocks(q), _to_blocks(cum), starts))
    return _from_blocks(out)


def _diff_attention(q1, q2, k1, k2, v, lam, slopes):
    s_len = q1.shape[2]
    scale = 1.0 / math.sqrt(q1.shape[-1])
    key_pos = jnp.arange(s_len)
    starts = jnp.arange(s_len // BLOCK_Q) * BLOCK_Q

    def one_block(args):
        q1i, q2i, st = args
        qpos = st + jnp.arange(BLOCK_Q)
        dist = (qpos[:, None] - key_pos[None, :]).astype(jnp.float32)
        alibi = -slopes[:, None, None] * dist
        mask = dist >= 0
        s1 = jnp.einsum('bhqd,bhkd->bhqk', q1i, k1, preferred_element_type=jnp.float32) * scale + alibi
        s2 = jnp.einsum('bhqd,bhkd->bhqk', q2i, k2, preferred_element_type=jnp.float32) * scale + alibi
        p = (jax.nn.softmax(jnp.where(mask, s1, -jnp.inf), axis=-1)
             - lam * jax.nn.softmax(jnp.where(mask, s2, -jnp.inf), axis=-1))
        return jnp.einsum('bhqk,bhkd->bhqd', p.astype(v.dtype), v)

    out = lax.map(one_block, (_to_blocks(q1), _to_blocks(q2), starts))
    return _from_blocks(out)


def _modulate(x, g, shift, scale):
    return _rms(x, g) * (1.0 + scale[:, None, :]) + shift[:, None, :]


def setup_inputs(seed: int = 0) -> dict:
    key = jax.random.key(seed)
    ks = jax.random.split(key, 20)
    f32 = jnp.float32
    nrm = lambda k, shape, s: jax.random.normal(k, shape, f32) * s
    d = D_MODEL
    return {
        "x": nrm(ks[0], (BATCH, SEQ, d), 1.0),
        "c": nrm(ks[1], (BATCH, d), 1.0),
        "ln1_g": 1.0 + nrm(ks[2], (DEPTH, d), 0.02),
        "ln2_g": 1.0 + nrm(ks[3], (DEPTH, d), 0.02),
        "w_ada": nrm(ks[4], (DEPTH, d, 6 * d), 0.5 * d ** -0.5),
        "b_ada": nrm(ks[5], (DEPTH, 6 * d), 0.02),
        "w_in": nrm(ks[6], (DEPTH, d, IN_WIDTH), d ** -0.5),
        "b_f": 1.5 + nrm(ks[7], (DEPTH, FOX_HEADS), 0.1),
        "fox_qk_g": 1.0 + nrm(ks[8], (DEPTH, 2, HEAD_DIM), 0.02),
        "diff_qk_g": 1.0 + nrm(ks[9], (DEPTH, 2, HEAD_DIM), 0.02),
        "diff_lam": nrm(ks[10], (DEPTH, 4, HEAD_DIM), 0.1),
        "diff_norm_g": 1.0 + nrm(ks[11], (DEPTH, 2 * HEAD_DIM), 0.02),
        "w_out": nrm(ks[12], (DEPTH, MIX_WIDTH, d), MIX_WIDTH ** -0.5),
        "w_gate": nrm(ks[13], (DEPTH, d, D_FF), d ** -0.5),
        "w_up": nrm(ks[14], (DEPTH, d, D_FF), d ** -0.5),
        "w_down": nrm(ks[15], (DEPTH, D_FF, d), D_FF ** -0.5),
    }


def reference(x, c, ln1_g, ln2_g, w_ada, b_ada, w_in, b_f, fox_qk_g, diff_qk_g,
              diff_lam, diff_norm_g, w_out, w_gate, w_up, w_down):
    b, s, d = x.shape
    cond = jax.nn.silu(c)
    slopes = 2.0 ** (-8.0 * jnp.arange(1, DIFF_HEADS + 1, dtype=jnp.float32) / DIFF_HEADS)
    splits = np.cumsum([FOX_WIDTH, FOX_WIDTH, FOX_WIDTH, FOX_HEADS,
                        DIFF_WIDTH, DIFF_WIDTH]).tolist()

    for l in range(DEPTH):
        mod = cond @ w_ada[l] + b_ada[l]
        sh1, sc1, g1, sh2, sc2, g2 = jnp.split(mod, 6, axis=-1)

        h = _modulate(x, ln1_g[l], sh1, sc1)
        u = h @ w_in[l]
        fq, fk, fv, fg, dq, dk, dv = jnp.split(u, splits, axis=-1)

        heads = lambda t, n, hd: t.reshape(b, s, n, hd).transpose(0, 2, 1, 3)
        fq = _rms(heads(fq, FOX_HEADS, HEAD_DIM), fox_qk_g[l, 0])
        fk = _rms(heads(fk, FOX_HEADS, HEAD_DIM), fox_qk_g[l, 1])
        fv = heads(fv, FOX_HEADS, HEAD_DIM)
        log_f = jax.nn.log_sigmoid(fg.astype(jnp.float32) + b_f[l].astype(jnp.float32))
        log_f = log_f.transpose(0, 2, 1)
        fox_out = _fox_attention(fq, fk, fv, log_f)

        dq = dq.reshape(b, s, DIFF_HEADS, 2, HEAD_DIM).transpose(0, 2, 3, 1, 4)
        dk = dk.reshape(b, s, DIFF_HEADS, 2, HEAD_DIM).transpose(0, 2, 3, 1, 4)
        dq = _rms(dq, diff_qk_g[l, 0])
        dk = _rms(dk, diff_qk_g[l, 1])
        dv = heads(dv, DIFF_HEADS, 2 * HEAD_DIM)
        lam_init = 0.8 - 0.6 * math.exp(-0.3 * l)
        lv = diff_lam[l].astype(jnp.float32)
        lam = jnp.exp(jnp.sum(lv[0] * lv[1])) - jnp.exp(jnp.sum(lv[2] * lv[3])) + lam_init
        diff_out = _diff_attention(dq[:, :, 0], dq[:, :, 1], dk[:, :, 0], dk[:, :, 1], dv, lam, slopes)
        diff_out = _rms(diff_out, diff_norm_g[l]) * (1.0 - lam_init)

        mixed = jnp.concatenate([
            fox_out.transpose(0, 2, 1, 3).reshape(b, s, FOX_WIDTH),
            diff_out.transpose(0, 2, 1, 3).reshape(b, s, DIFF_WIDTH)], axis=-1)
        x = x + g1[:, None, :] * (mixed @ w_out[l])

        h2 = _modulate(x, ln2_g[l], sh2, sc2)
        y = (jax.nn.silu(h2 @ w_gate[l]) * (h2 @ w_up[l])) @ w_down[l]
        x = x + g2[:, None, :] * y
    return x
```

```python
import functools
import math

import jax
import jax.numpy as jnp
from jax import lax
from jax.experimental import pallas as pl
from jax.experimental.pallas import tpu as pltpu

D_MODEL = 1024
HEAD_DIM = 64
FOX_HEADS = 8
DIFF_HEADS = 4
FOX_WIDTH = FOX_HEADS * HEAD_DIM
DIFF_WIDTH = DIFF_HEADS * 2 * HEAD_DIM
MOD_CHUNKS = 6
EPS = 1e-6

LANES = 128
BF16_SUBLANES = 16
LOG2E = math.log2(math.e)
QK_SCALE = LOG2E / math.sqrt(HEAD_DIM)
NEG = -1e30

TOKEN_TILE = 512
ATTN_TILE = 256
FFN_CHUNKS = 2
VMEM_LIMIT = 56 * 1024 * 1024

BF16 = jnp.bfloat16
F32 = jnp.float32

_NT = (((1,), (1,)), ((), ()))


def _dot(a, b):
    return jnp.dot(a, b, preferred_element_type=F32)


def _rms_modulate(x, gain, shift, scale):
    ms = jnp.mean(x * x, axis=-1, keepdims=True)
    return (x * lax.rsqrt(ms + EPS) * gain) * (1.0 + scale) + shift


def _ada_kernel(c_ref, w_ref, b_ref, o_ref):
    c = c_ref[...]
    cond = c / (1.0 + jnp.exp(-c))
    o_ref[...] = _dot(cond.astype(BF16), w_ref[...].astype(BF16)) + b_ref[...]


def _ada(c, w_ada, b_ada):
    depth, d, n = w_ada.shape
    batch = c.shape[0]
    nb = n // d
    return pl.pallas_call(
        _ada_kernel,
        grid=(depth, nb),
        in_specs=[
            pl.BlockSpec((batch, d), lambda l, j: (0, 0)),
            pl.BlockSpec((None, d, d), lambda l, j: (l, 0, j)),
            pl.BlockSpec((None, 1, d), lambda l, j: (l, 0, j)),
        ],
        out_specs=pl.BlockSpec((None, batch, d), lambda l, j: (l, 0, j)),
        out_shape=jax.ShapeDtypeStruct((depth, batch, n), F32),
        name="ada_mod",
    )(c, w_ada, b_ada.reshape(depth, 1, n))


def _head_rms(u, gain):
    rows, width = u.shape
    lane = lax.broadcasted_iota(jnp.int32, (rows, LANES), 1)
    lo = lane < HEAD_DIM
    outs = []
    for g in range(width // LANES):
        blk = u[:, g * LANES:(g + 1) * LANES]
        sq = blk * blk
        ss_lo = jnp.sum(jnp.where(lo, sq, 0.0), axis=-1, keepdims=True)
        ss_hi = jnp.sum(jnp.where(lo, 0.0, sq), axis=-1, keepdims=True)
        r = jnp.where(lo, lax.rsqrt(ss_lo / HEAD_DIM + EPS), lax.rsqrt(ss_hi / HEAD_DIM + EPS))
        outs.append(blk * r * gain[:, g * LANES:(g + 1) * LANES])
    return jnp.concatenate(outs, axis=-1)


def _inproj_kernel(x_ref, g_ref, sh_ref, sc_ref, wf_ref, wg_ref, wd_ref, bf_ref,
                   fqg_ref, fkg_ref, dqg_ref, dkg_ref,
                   fq_ref, fk_ref, fv_ref, kb_ref, dq_ref, dk_ref, dv_ref, carry_ref):
    si = pl.program_id(1)
    tm = x_ref.shape[0]
    h = _rms_modulate(x_ref[...], g_ref[...], sh_ref[...], sc_ref[...]).astype(BF16)

    w = FOX_WIDTH
    fq_ref[...] = _head_rms(_dot(h, wf_ref[:, 0:w]), fqg_ref[...] * QK_SCALE).astype(BF16)
    fk_ref[...] = _head_rms(_dot(h, wf_ref[:, w:2 * w]), fkg_ref[...]).astype(BF16)
    fv_ref[...] = _dot(h, wf_ref[:, 2 * w:3 * w]).astype(BF16)
    w = DIFF_WIDTH
    dq_ref[...] = _head_rms(_dot(h, wd_ref[:, 0:w]), dqg_ref[...] * QK_SCALE).astype(BF16)
    dk_ref[...] = _head_rms(_dot(h, wd_ref[:, w:2 * w]), dkg_ref[...]).astype(BF16)
    dv_ref[...] = _dot(h, wd_ref[:, 2 * w:3 * w]).astype(BF16)

    z = lax.dot_general(wg_ref[...], h, _NT, preferred_element_type=F32) + bf_ref[...]
    logf = jnp.minimum(z, 0.0) - jnp.log1p(jnp.exp(-jnp.abs(z)))

    hi = logf.astype(BF16)
    r1 = logf - hi.astype(F32)
    mid = r1.astype(BF16)
    low = (r1 - mid.astype(F32)).astype(BF16)
    src = lax.broadcasted_iota(jnp.int32, (tm, tm), 0)
    dst = lax.broadcasted_iota(jnp.int32, (tm, tm), 1)
    tri = jnp.where(src <= dst, 1.0, 0.0).astype(BF16)
    local = _dot(hi, tri) + _dot(mid, tri) + _dot(low, tri)

    @pl.when(si == 0)
    def _():
        carry_ref[...] = jnp.zeros_like(carry_ref)

    carry = carry_ref[:, 0:1]
    cum = carry + local
    kb_ref[...] = (-LOG2E) * cum[0:FOX_HEADS, :]
    carry_ref[...] = jnp.broadcast_to(carry + jnp.sum(logf, axis=-1, keepdims=True), carry_ref.shape)


def _inproj(x, gain, shift, scale, w_f, w_gt, w_d, b_f, fqg, fkg, dqg, dkg):
    batch, seq, d = x.shape
    tm = TOKEN_TILE
    tok = lambda width: pl.BlockSpec((None, tm, width), lambda b, s: (b, s, 0))
    vec = lambda width: pl.BlockSpec((None, 1, width), lambda b, s: (b, 0, 0))
    const = lambda shape: pl.BlockSpec(shape, lambda b, s: (0,) * len(shape),
                                       pipeline_mode=pl.Buffered(1))
    act = lambda width: jax.ShapeDtypeStruct((batch, seq, width), BF16)
    return pl.pallas_call(
        _inproj_kernel,
        grid=(batch, seq // tm),
        in_specs=[
            tok(d), const((1, d)), vec(d), vec(d),
            const(w_f.shape), const(w_gt.shape), const(w_d.shape), const(b_f.shape),
            const((1, FOX_WIDTH)), const((1, FOX_WIDTH)), const((1, DIFF_WIDTH)), const((1, DIFF_WIDTH)),
        ],
        out_specs=[
            tok(FOX_WIDTH), tok(FOX_WIDTH), tok(FOX_WIDTH),
            pl.BlockSpec((None, FOX_HEADS, tm), lambda b, s: (b, 0, s)),
            tok(DIFF_WIDTH), tok(DIFF_WIDTH), tok(DIFF_WIDTH),
        ],
        out_shape=[
            act(FOX_WIDTH), act(FOX_WIDTH), act(FOX_WIDTH),
            jax.ShapeDtypeStruct((batch, FOX_HEADS, seq), F32),
            act(DIFF_WIDTH), act(DIFF_WIDTH), act(DIFF_WIDTH),
        ],
        scratch_shapes=[pltpu.VMEM((BF16_SUBLANES, LANES), F32)],
        compiler_params=pltpu.CompilerParams(
            dimension_semantics=("arbitrary", "arbitrary"), vmem_limit_bytes=VMEM_LIMIT),
        name="inproj",
    )(x, gain, shift, scale, w_f, w_gt, w_d, b_f, fqg, fkg, dqg, dkg)


def _softmax_step(s, m, l):
    m_new = jnp.maximum(m, jnp.max(s, axis=-1, keepdims=True))
    p = jnp.exp2(s - m_new)
    alpha = jnp.exp2(m - m_new)
    return p, alpha, m_new, alpha * l + jnp.sum(p, axis=-1, keepdims=True)


def _causal(s):
    row = lax.broadcasted_iota(jnp.int32, s.shape, 0)
    col = lax.broadcasted_iota(jnp.int32, s.shape, 1)
    return jnp.where(col <= row, s, NEG)


def _split_halves(q):
    lane = lax.broadcasted_iota(jnp.int32, q.shape, 1)
    lo = lane < HEAD_DIM
    zero = jnp.zeros_like(q)
    return lo, jnp.where(lo, q, zero), jnp.where(lo, zero, q)


def _fox_kernel(q_ref, k_ref, v_ref, kb_ref, o_ref):
    qi = pl.program_id(2)
    t = q_ref.shape[0]
    lo, q0, q1 = _split_halves(q_ref[...])
    qs = (q0, q1)

    def step(j, carry, diagonal):
        m, l, acc = carry
        start = pl.multiple_of(j * t, t)
        k = k_ref[pl.ds(start, t), :]
        v = v_ref[pl.ds(start, t), :]
        kb = kb_ref[:, pl.ds(start, t)]
        m_out, l_out, alphas, pvs = [], [], [], []
        for hh in range(2):
            s = lax.dot_general(qs[hh], k, _NT, preferred_element_type=F32) + kb[hh:hh + 1, :]
            if diagonal:
                s = _causal(s)
            p, alpha, m_new, l_new = _softmax_step(s, m[hh], l[hh])
            m_out.append(m_new)
            l_out.append(l_new)
            alphas.append(alpha)
            pvs.append(_dot(p.astype(BF16), v))
        acc = acc * jnp.where(lo, alphas[0], alphas[1]) + jnp.where(lo, pvs[0], pvs[1])
        return tuple(m_out), tuple(l_out), acc

    col = jnp.full((t, 1), NEG, F32)
    init = ((col, col), (jnp.zeros((t, 1), F32),) * 2, jnp.zeros((t, LANES), F32))
    carry = lax.fori_loop(0, qi, functools.partial(step, diagonal=False), init)
    _, l, acc = step(qi, carry, diagonal=True)
    o_ref[...] = (acc / jnp.where(lo, l[0], l[1])).astype(o_ref.dtype)


def _fox_attention(q, k, v, kb):
    batch, seq, width = q.shape
    t = ATTN_TILE
    pairs = width // LANES
    kb = kb.reshape(batch, pairs, 2, seq)
    return pl.pallas_call(
        _fox_kernel,
        grid=(batch, pairs, seq // t),
        in_specs=[
            pl.BlockSpec((None, t, LANES), lambda b, h, i: (b, i, h)),
            pl.BlockSpec((None, seq, LANES), lambda b, h, i: (b, 0, h)),
            pl.BlockSpec((None, seq, LANES), lambda b, h, i: (b, 0, h)),
            pl.BlockSpec((None, None, 2, seq), lambda b, h, i: (b, h, 0, 0)),
        ],
        out_specs=pl.BlockSpec((None, t, LANES), lambda b, h, i: (b, i, h)),
        out_shape=jax.ShapeDtypeStruct((batch, seq, width), BF16),
        compiler_params=pltpu.CompilerParams(
            dimension_semantics=("arbitrary", "arbitrary", "arbitrary")),
        name="fox_attn",
    )(q, k, v, kb)


def _diff_kernel(q_ref, k_ref, v_ref, lam_ref, ng_ref, o_ref, *, lam_init):
    head = pl.program_id(1)
    qi = pl.program_id(2)
    t = q_ref.shape[0]
    _, q1, q2 = _split_halves(q_ref[...])
    qs = (q1, q2)
    slope = jnp.exp2(-2.0 * (jnp.full((1, 1), head, jnp.int32) + 1).astype(F32))

    def step(j, carry, diagonal):
        m, l, acc = carry
        start = pl.multiple_of(j * t, t)
        k = k_ref[pl.ds(start, t), :]
        v = v_ref[pl.ds(start, t), :]
        pos = (start + lax.broadcasted_iota(jnp.int32, (1, t), 1)).astype(F32)
        kb = (LOG2E * slope) * pos
        m_out, l_out, acc_out = [], [], []
        for hh in range(2):
            s = lax.dot_general(qs[hh], k, _NT, preferred_element_type=F32) + kb
            if diagonal:
                s = _causal(s)
            p, alpha, m_new, l_new = _softmax_step(s, m[hh], l[hh])
            m_out.append(m_new)
            l_out.append(l_new)
            acc_out.append(acc[hh] * alpha + _dot(p.astype(BF16), v))
        return tuple(m_out), tuple(l_out), tuple(acc_out)

    col = jnp.full((t, 1), NEG, F32)
    init = ((col, col), (jnp.zeros((t, 1), F32),) * 2, (jnp.zeros((t, LANES), F32),) * 2)
    carry = lax.fori_loop(0, qi, functools.partial(step, diagonal=False), init)
    _, l, acc = step(qi, carry, diagonal=True)

    lv = lam_ref[...]
    lam = (jnp.exp(jnp.sum(lv[0:1] * lv[1:2], axis=-1, keepdims=True))
           - jnp.exp(jnp.sum(lv[2:3] * lv[3:4], axis=-1, keepdims=True)) + lam_init)
    o = acc[0] / l[0] - lam * (acc[1] / l[1])
    ms = jnp.mean(o * o, axis=-1, keepdims=True)
    o_ref[...] = ((o * lax.rsqrt(ms + EPS) * ng_ref[...]) * (1.0 - lam_init)).astype(o_ref.dtype)


def _diff_attention(q, k, v, lam_params, norm_gain, lam_init):
    batch, seq, width = q.shape
    t = ATTN_TILE
    heads = width // LANES
    return pl.pallas_call(
        functools.partial(_diff_kernel, lam_init=lam_init),
        grid=(batch, heads, seq // t),
        in_specs=[
            pl.BlockSpec((None, t, LANES), lambda b, h, i: (b, i, h)),
            pl.BlockSpec((None, seq, LANES), lambda b, h, i: (b, 0, h)),
            pl.BlockSpec((None, seq, LANES), lambda b, h, i: (b, 0, h)),
            pl.BlockSpec(lam_params.shape, lambda b, h, i: (0, 0)),
            pl.BlockSpec(norm_gain.shape, lambda b, h, i: (0, 0)),
        ],
        out_specs=pl.BlockSpec((None, t, LANES), lambda b, h, i: (b, i, h)),
        out_shape=jax.ShapeDtypeStruct((batch, seq, width), BF16),
        compiler_params=pltpu.CompilerParams(
            dimension_semantics=("arbitrary", "arbitrary", "arbitrary")),
        name="diff_attn",
    )(q, k, v, lam_params, norm_gain)


def _mlp_kernel(fox_ref, diff_ref, x_ref, wo_ref, g1_ref, ln_ref, sh_ref, sc_ref, g2_ref,
                wg_ref, wu_ref, wd_ref, o_ref):
    mixed = _dot(fox_ref[...], wo_ref[0:FOX_WIDTH, :]) + _dot(diff_ref[...], wo_ref[FOX_WIDTH:, :])
    x1 = x_ref[...] + g1_ref[...] * mixed
    h = _rms_modulate(x1, ln_ref[...], sh_ref[...], sc_ref[...]).astype(BF16)
    d_ff = wg_ref.shape[1]
    chunk = d_ff // FFN_CHUNKS
    y = None
    for c in range(FFN_CHUNKS):
        cols = slice(c * chunk, (c + 1) * chunk)
        gate = _dot(h, wg_ref[:, cols])
        up = _dot(h, wu_ref[:, cols])
        a = ((gate / (1.0 + jnp.exp(-gate))) * up).astype(BF16)
        part = _dot(a, wd_ref[cols, :])
        y = part if y is None else y + part
    o_ref[...] = x1 + g2_ref[...] * y


def _mlp(fox, diff, x, w_o, g1, ln_g, shift, scale, g2, w_gate, w_up, w_down):
    batch, seq, d = x.shape
    tm = TOKEN_TILE
    tok = lambda width: pl.BlockSpec((None, tm, width), lambda b, s: (b, s, 0))
    vec = lambda width: pl.BlockSpec((None, 1, width), lambda b, s: (b, 0, 0))
    const = lambda shape: pl.BlockSpec(shape, lambda b, s: (0,) * len(shape),
                                       pipeline_mode=pl.Buffered(1))
    return pl.pallas_call(
        _mlp_kernel,
        grid=(batch, seq // tm),
        in_specs=[
            tok(FOX_WIDTH), tok(DIFF_WIDTH), tok(d), const(w_o.shape), vec(d), const((1, d)),
            vec(d), vec(d), vec(d), const(w_gate.shape), const(w_up.shape), const(w_down.shape),
        ],
        out_specs=tok(d),
        out_shape=jax.ShapeDtypeStruct((batch, seq, d), F32),
        compiler_params=pltpu.CompilerParams(
            dimension_semantics=("arbitrary", "arbitrary"), vmem_limit_bytes=VMEM_LIMIT),
        name="out_mlp",
    )(fox, diff, x, w_o, g1, ln_g, shift, scale, g2, w_gate, w_up, w_down)


def kernel(x, c, ln1_g, ln2_g, w_ada, b_ada, w_in, b_f, fox_qk_g, diff_qk_g, diff_lam, diff_norm_g,
           w_out, w_gate, w_up, w_down):
    depth = w_in.shape[0]
    batch, _, d = x.shape
    mod = _ada(c, w_ada, b_ada)
    fox_cols = 3 * FOX_WIDTH
    for l in range(depth):
        sh1, sc1, g1, sh2, sc2, g2 = (
            mod[l, :, i * d:(i + 1) * d].reshape(batch, 1, d) for i in range(MOD_CHUNKS))
        w_f = w_in[l, :, :fox_cols].astype(BF16)
        w_d = w_in[l, :, fox_cols + FOX_HEADS:].astype(BF16)
        w_gt = jnp.zeros((BF16_SUBLANES, d), BF16).at[:FOX_HEADS].set(
            w_in[l, :, fox_cols:fox_cols + FOX_HEADS].T.astype(BF16))
        b_fp = jnp.zeros((BF16_SUBLANES, 1), F32).at[:FOX_HEADS, 0].set(b_f[l])
        per_head = lambda g, n: jnp.tile(g, n).reshape(1, n * HEAD_DIM)
        fq, fk, fv, kb, dq, dk, dv = _inproj(
            x, ln1_g[l].reshape(1, d), sh1, sc1, w_f, w_gt, w_d, b_fp,
            per_head(fox_qk_g[l, 0], FOX_HEADS), per_head(fox_qk_g[l, 1], FOX_HEADS),
            per_head(diff_qk_g[l, 0], 2 * DIFF_HEADS), per_head(diff_qk_g[l, 1], 2 * DIFF_HEADS))
        fox = _fox_attention(fq, fk, fv, kb)
        lam_init = 0.8 - 0.6 * math.exp(-0.3 * l)
        diff = _diff_attention(dq, dk, dv, diff_lam[l], diff_norm_g[l].reshape(1, 2 * HEAD_DIM), lam_init)
        x = _mlp(fox, diff, x, w_out[l].astype(BF16), g1, ln2_g[l].reshape(1, d), sh2, sc2, g2,
                 w_gate[l].astype(BF16), w_up[l].astype(BF16), w_down[l].astype(BF16))
    return x
```

```python
import functools
import math

import jax
import jax.numpy as jnp
from jax import lax
from jax.experimental import pallas as pl
from jax.experimental.pallas import tpu as pltpu

D_MODEL = 1024
HEAD_DIM = 64
FOX_HEADS = 8
DIFF_HEADS = 4
FOX_WIDTH = FOX_HEADS * HEAD_DIM
DIFF_WIDTH = DIFF_HEADS * 2 * HEAD_DIM
MOD_CHUNKS = 6
EPS = 1e-6

LANES = 128
BF16_SUBLANES = 16
LOG2E = math.log2(math.e)
QK_SCALE = LOG2E / math.sqrt(HEAD_DIM)
NEG = -1e30

TOKEN_TILE = 512
ATTN_TILE = 256
FFN_CHUNKS = 2
VMEM_LIMIT = 56 * 1024 * 1024

BF16 = jnp.bfloat16
F32 = jnp.float32

_NT = (((1,), (1,)), ((), ()))


def _dot(a, b):
    return jnp.dot(a, b, preferred_element_type=F32)


def _rms_modulate(x, gain, shift, scale):
    ms = jnp.mean(x * x, axis=-1, keepdims=True)
    return (x * lax.rsqrt(ms + EPS) * gain) * (1.0 + scale) + shift


def _ada_kernel(c_ref, w_ref, b_ref, o_ref):
    c = c_ref[...]
    cond = c / (1.0 + jnp.exp(-c))
    o_ref[...] = _dot(cond.astype(BF16), w_ref[...].astype(BF16)) + b_ref[...]


def _ada(c, w_ada, b_ada):
    depth, d, n = w_ada.shape
    batch = c.shape[0]
    nb = n // d
    return pl.pallas_call(
        _ada_kernel,
        grid=(depth, nb),
        in_specs=[
            pl.BlockSpec((batch, d), lambda l, j: (0, 0)),
            pl.BlockSpec((None, d, d), lambda l, j: (l, 0, j)),
            pl.BlockSpec((None, 1, d), lambda l, j: (l, 0, j)),
        ],
        out_specs=pl.BlockSpec((None, batch, d), lambda l, j: (l, 0, j)),
        out_shape=jax.ShapeDtypeStruct((depth, batch, n), F32),
        name="ada_mod",
    )(c, w_ada, b_ada.reshape(depth, 1, n))


def _head_rms(u, gain):
    rows, width = u.shape
    lane = lax.broadcasted_iota(jnp.int32, (rows, LANES), 1)
    lo = lane < HEAD_DIM
    outs = []
    for g in range(width // LANES):
        blk = u[:, g * LANES:(g + 1) * LANES]
        sq = blk * blk
        ss_lo = jnp.sum(jnp.where(lo, sq, 0.0), axis=-1, keepdims=True)
        ss_hi = jnp.sum(jnp.where(lo, 0.0, sq), axis=-1, keepdims=True)
        r = jnp.where(lo, lax.rsqrt(ss_lo / HEAD_DIM + EPS), lax.rsqrt(ss_hi / HEAD_DIM + EPS))
        outs.append(blk * r * gain[:, g * LANES:(g + 1) * LANES])
    return jnp.concatenate(outs, axis=-1)


def _inproj_kernel(x_ref, g_ref, sh_ref, sc_ref, wf_ref, wg_ref, wd_ref, bf_ref,
                   fqg_ref, fkg_ref, dqg_ref, dkg_ref,
                   fq_ref, fk_ref, fv_ref, kb_ref, dq_ref, dk_ref, dv_ref, carry_ref):
    si = pl.program_id(1)
    tm = x_ref.shape[0]
    h = _rms_modulate(x_ref[...], g_ref[...], sh_ref[...], sc_ref[...]).astype(BF16)

    w = FOX_WIDTH
    fq_ref[...] = _head_rms(_dot(h, wf_ref[:, 0:w]), fqg_ref[...] * QK_SCALE).astype(BF16)
    fk_ref[...] = _head_rms(_dot(h, wf_ref[:, w:2 * w]), fkg_ref[...]).astype(BF16)
    fv_ref[...] = _dot(h, wf_ref[:, 2 * w:3 * w]).astype(BF16)
    w = DIFF_WIDTH
    dq_ref[...] = _head_rms(_dot(h, wd_ref[:, 0:w]), dqg_ref[...] * QK_SCALE).astype(BF16)
    dk_ref[...] = _head_rms(_dot(h, wd_ref[:, w:2 * w]), dkg_ref[...]).astype(BF16)
    dv_ref[...] = _dot(h, wd_ref[:, 2 * w:3 * w]).astype(BF16)

    z = lax.dot_general(wg_ref[...], h, _NT, preferred_element_type=F32) + bf_ref[...]
    logf = jnp.minimum(z, 0.0) - jnp.log1p(jnp.exp(-jnp.abs(z)))

    hi = logf.astype(BF16)
    r1 = logf - hi.astype(F32)
    mid = r1.astype(BF16)
    low = (r1 - mid.astype(F32)).astype(BF16)
    src = lax.broadcasted_iota(jnp.int32, (tm, tm), 0)
    dst = lax.broadcasted_iota(jnp.int32, (tm, tm), 1)
    tri = jnp.where(src <= dst, 1.0, 0.0).astype(BF16)
    local = _dot(hi, tri) + _dot(mid, tri) + _dot(low, tri)

    @pl.when(si == 0)
    def _():
        carry_ref[...] = jnp.zeros_like(carry_ref)

    carry = carry_ref[:, 0:1]
    cum = carry + local
    kb_ref[...] = (-LOG2E) * cum[0:FOX_HEADS, :]
    carry_ref[...] = jnp.broadcast_to(carry + jnp.sum(logf, axis=-1, keepdims=True), carry_ref.shape)


def _inproj(x, gain, shift, scale, w_f, w_gt, w_d, b_f, fqg, fkg, dqg, dkg):
    batch, seq, d = x.shape
    tm = TOKEN_TILE
    tok = lambda width: pl.BlockSpec((None, tm, width), lambda b, s: (b, s, 0))
    vec = lambda width: pl.BlockSpec((None, 1, width), lambda b, s: (b, 0, 0))
    const = lambda shape: pl.BlockSpec(shape, lambda b, s: (0,) * len(shape),
                                       pipeline_mode=pl.Buffered(1))
    act = lambda width: jax.ShapeDtypeStruct((batch, seq, width), BF16)
    return pl.pallas_call(
        _inproj_kernel,
        grid=(batch, seq // tm),
        in_specs=[
            tok(d), const((1, d)), vec(d), vec(d),
            const(w_f.shape), const(w_gt.shape), const(w_d.shape), const(b_f.shape),
            const((1, FOX_WIDTH)), const((1, FOX_WIDTH)), const((1, DIFF_WIDTH)), const((1, DIFF_WIDTH)),
        ],
        out_specs=[
            tok(FOX_WIDTH), tok(FOX_WIDTH), tok(FOX_WIDTH),
            pl.BlockSpec((None, FOX_HEADS, tm), lambda b, s: (b, 0, s)),
            tok(DIFF_WIDTH), tok(DIFF_WIDTH), tok(DIFF_WIDTH),
        ],
        out_shape=[
            act(FOX_WIDTH), act(FOX_WIDTH), act(FOX_WIDTH),
            jax.ShapeDtypeStruct((batch, FOX_HEADS, seq), F32),
            act(DIFF_WIDTH), act(DIFF_WIDTH), act(DIFF_WIDTH),
        ],
        scratch_shapes=[pltpu.VMEM((BF16_SUBLANES, LANES), F32)],
        compiler_params=pltpu.CompilerParams(
            dimension_semantics=("arbitrary", "arbitrary"), vmem_limit_bytes=VMEM_LIMIT),
        name="inproj",
    )(x, gain, shift, scale, w_f, w_gt, w_d, b_f, fqg, fkg, dqg, dkg)


def _causal(s):
    row = lax.broadcasted_iota(jnp.int32, s.shape, 0)
    col = lax.broadcasted_iota(jnp.int32, s.shape, 1)
    return jnp.where(col <= row, s, NEG)


def _split_halves(q):
    lane = lax.broadcasted_iota(jnp.int32, q.shape, 1)
    lo = lane < HEAD_DIM
    zero = jnp.zeros_like(q)
    return lo, jnp.where(lo, q, zero), jnp.where(lo, zero, q)


def _softmax_update(s2, m2, l2):
    ps, alphas, ms, ls = [], [], [], []
    for s, m, l in zip(s2, m2, l2):
        m_new = jnp.maximum(m, jnp.max(s, axis=-1, keepdims=True))
        p = jnp.exp2(s - m_new)
        alpha = jnp.exp2(m - m_new)
        part = p[:, 0:LANES]
        for c in range(1, s.shape[1] // LANES):
            part = part + p[:, c * LANES:(c + 1) * LANES]
        ps.append(p.astype(BF16))
        alphas.append(alpha)
        ms.append(m_new)
        ls.append(alpha * l + part)
    return tuple(ps), tuple(alphas), tuple(ms), tuple(ls)


def _flash_scratch(t):
    return [pltpu.VMEM((2, t, t), F32), pltpu.VMEM((2, t, t), BF16)]


def _flash(qs, k_ref, v_ref, s_scr, p_scr, bias_fn, accumulate, acc0, qi, t):
    def scores(j, diagonal):
        start = pl.multiple_of(j * t, t)
        k = k_ref[pl.ds(start, t), :]
        out = []
        for q, bias in zip(qs, bias_fn(start)):
            s = lax.dot_general(q, k, _NT, preferred_element_type=F32) + bias
            out.append(_causal(s) if diagonal else s)
        return tuple(out)

    def values(p2, alpha2, acc, j):
        start = pl.multiple_of(j * t, t)
        return accumulate(acc, alpha2, p2, v_ref[pl.ds(start, t), :])

    def put(ref, pair):
        for n, val in enumerate(pair):
            ref[n] = val

    neg = jnp.full((t, 1), NEG, F32)
    zero = jnp.zeros((t, LANES), F32)
    p2, alpha2, m2, l2 = _softmax_update(scores(qi, True), (neg, neg), (zero, zero))
    put(p_scr, p2)
    put(s_scr, scores(0, False))

    def body(i, carry):
        alpha2, pending, m2, l2, acc = carry
        s_next = scores(jnp.minimum(i + 1, qi - 1), False)
        acc = values((p_scr[0], p_scr[1]), alpha2, acc, pending)
        p2, alpha2, m2, l2 = _softmax_update((s_scr[0], s_scr[1]), m2, l2)
        put(s_scr, s_next)
        put(p_scr, p2)
        return alpha2, i, m2, l2, acc

    alpha2, pending, _, l2, acc = lax.fori_loop(0, qi, body, (alpha2, qi, m2, l2, acc0))
    acc = values((p_scr[0], p_scr[1]), alpha2, acc, pending)
    return acc, tuple(jnp.sum(l, axis=-1, keepdims=True) for l in l2)


def _fox_kernel(q_ref, k_ref, v_ref, kb_ref, o_ref, s_scr, p_scr):
    t = q_ref.shape[0]
    lo, q0, q1 = _split_halves(q_ref[...])

    def bias_fn(start):
        kb = kb_ref[:, pl.ds(start, t)]
        return kb[0:1, :], kb[1:2, :]

    def accumulate(acc, alpha2, p2, v):
        pv = jnp.where(lo, _dot(p2[0], v), _dot(p2[1], v))
        return acc * jnp.where(lo, alpha2[0], alpha2[1]) + pv

    acc, l2 = _flash((q0, q1), k_ref, v_ref, s_scr, p_scr, bias_fn, accumulate,
                     jnp.zeros((t, LANES), F32), pl.program_id(2), t)
    o_ref[...] = (acc / jnp.where(lo, l2[0], l2[1])).astype(o_ref.dtype)


def _fox_attention(q, k, v, kb):
    batch, seq, width = q.shape
    t = ATTN_TILE
    pairs = width // LANES
    kb = kb.reshape(batch, pairs, 2, seq)
    return pl.pallas_call(
        _fox_kernel,
        grid=(batch, pairs, seq // t),
        in_specs=[
            pl.BlockSpec((None, t, LANES), lambda b, h, i: (b, i, h)),
            pl.BlockSpec((None, seq, LANES), lambda b, h, i: (b, 0, h)),
            pl.BlockSpec((None, seq, LANES), lambda b, h, i: (b, 0, h)),
            pl.BlockSpec((None, None, 2, seq), lambda b, h, i: (b, h, 0, 0)),
        ],
        out_specs=pl.BlockSpec((None, t, LANES), lambda b, h, i: (b, i, h)),
        out_shape=jax.ShapeDtypeStruct((batch, seq, width), BF16),
        scratch_shapes=_flash_scratch(t),
        compiler_params=pltpu.CompilerParams(
            dimension_semantics=("arbitrary", "arbitrary", "arbitrary")),
        name="fox_attn",
    )(q, k, v, kb)


def _diff_kernel(q_ref, k_ref, v_ref, lam_ref, ng_ref, o_ref, s_scr, p_scr, *, lam_init):
    head = pl.program_id(1)
    t = q_ref.shape[0]
    _, q1, q2 = _split_halves(q_ref[...])
    slope = jnp.exp2(-2.0 * (jnp.full((1, 1), head, jnp.int32) + 1).astype(F32))

    def bias_fn(start):
        pos = (start + lax.broadcasted_iota(jnp.int32, (1, t), 1)).astype(F32)
        kb = (LOG2E * slope) * pos
        return kb, kb

    def accumulate(acc, alpha2, p2, v):
        return tuple(a * alpha + _dot(p, v) for a, alpha, p in zip(acc, alpha2, p2))

    acc, l = _flash((q1, q2), k_ref, v_ref, s_scr, p_scr, bias_fn, accumulate,
                    (jnp.zeros((t, LANES), F32),) * 2, pl.program_id(2), t)

    lv = lam_ref[...]
    lam = (jnp.exp(jnp.sum(lv[0:1] * lv[1:2], axis=-1, keepdims=True))
           - jnp.exp(jnp.sum(lv[2:3] * lv[3:4], axis=-1, keepdims=True)) + lam_init)
    o = acc[0] / l[0] - lam * (acc[1] / l[1])
    ms = jnp.mean(o * o, axis=-1, keepdims=True)
    o_ref[...] = ((o * lax.rsqrt(ms + EPS) * ng_ref[...]) * (1.0 - lam_init)).astype(o_ref.dtype)


def _diff_attention(q, k, v, lam_params, norm_gain, lam_init):
    batch, seq, width = q.shape
    t = ATTN_TILE
    heads = width // LANES
    return pl.pallas_call(
        functools.partial(_diff_kernel, lam_init=lam_init),
        grid=(batch, heads, seq // t),
        in_specs=[
            pl.BlockSpec((None, t, LANES), lambda b, h, i: (b, i, h)),
            pl.BlockSpec((None, seq, LANES), lambda b, h, i: (b, 0, h)),
            pl.BlockSpec((None, seq, LANES), lambda b, h, i: (b, 0, h)),
            pl.BlockSpec(lam_params.shape, lambda b, h, i: (0, 0)),
            pl.BlockSpec(norm_gain.shape, lambda b, h, i: (0, 0)),
        ],
        out_specs=pl.BlockSpec((None, t, LANES), lambda b, h, i: (b, i, h)),
        out_shape=jax.ShapeDtypeStruct((batch, seq, width), BF16),
        scratch_shapes=_flash_scratch(t),
        compiler_params=pltpu.CompilerParams(
            dimension_semantics=("arbitrary", "arbitrary", "arbitrary")),
        name="diff_attn",
    )(q, k, v, lam_params, norm_gain)


def _mlp_kernel(fox_ref, diff_ref, x_ref, wo_ref, g1_ref, ln_ref, sh_ref, sc_ref, g2_ref,
                wg_ref, wu_ref, wd_ref, o_ref):
    mixed = _dot(fox_ref[...], wo_ref[0:FOX_WIDTH, :]) + _dot(diff_ref[...], wo_ref[FOX_WIDTH:, :])
    x1 = x_ref[...] + g1_ref[...] * mixed
    h = _rms_modulate(x1, ln_ref[...], sh_ref[...], sc_ref[...]).astype(BF16)
    d_ff = wg_ref.shape[1]
    chunk = d_ff // FFN_CHUNKS
    y = None
    for c in range(FFN_CHUNKS):
        cols = slice(c * chunk, (c + 1) * chunk)
        gate = _dot(h, wg_ref[:, cols])
        up = _dot(h, wu_ref[:, cols])
        a = ((gate / (1.0 + jnp.exp(-gate))) * up).astype(BF16)
        part = _dot(a, wd_ref[cols, :])
        y = part if y is None else y + part
    o_ref[...] = x1 + g2_ref[...] * y


def _mlp(fox, diff, x, w_o, g1, ln_g, shift, scale, g2, w_gate, w_up, w_down):
    batch, seq, d = x.shape
    tm = TOKEN_TILE
    tok = lambda width: pl.BlockSpec((None, tm, width), lambda b, s: (b, s, 0))
    vec = lambda width: pl.BlockSpec((None, 1, width), lambda b, s: (b, 0, 0))
    const = lambda shape: pl.BlockSpec(shape, lambda b, s: (0,) * len(shape),
                                       pipeline_mode=pl.Buffered(1))
    return pl.pallas_call(
        _mlp_kernel,
        grid=(batch, seq // tm),
        in_specs=[
            tok(FOX_WIDTH), tok(DIFF_WIDTH), tok(d), const(w_o.shape), vec(d), const((1, d)),
            vec(d), vec(d), vec(d), const(w_gate.shape), const(w_up.shape), const(w_down.shape),
        ],
        out_specs=tok(d),
        out_shape=jax.ShapeDtypeStruct((batch, seq, d), F32),
        compiler_params=pltpu.CompilerParams(
            dimension_semantics=("arbitrary", "arbitrary"), vmem_limit_bytes=VMEM_LIMIT),
        name="out_mlp",
    )(fox, diff, x, w_o, g1, ln_g, shift, scale, g2, w_gate, w_up, w_down)


def kernel(x, c, ln1_g, ln2_g, w_ada, b_ada, w_in, b_f, fox_qk_g, diff_qk_g, diff_lam, diff_norm_g,
           w_out, w_gate, w_up, w_down):
    depth = w_in.shape[0]
    batch, _, d = x.shape
    mod = _ada(c, w_ada, b_ada)
    fox_cols = 3 * FOX_WIDTH
    for l in range(depth):
        sh1, sc1, g1, sh2, sc2, g2 = (
            mod[l, :, i * d:(i + 1) * d].reshape(batch, 1, d) for i in range(MOD_CHUNKS))
        w_f = w_in[l, :, :fox_cols].astype(BF16)
        w_d = w_in[l, :, fox_cols + FOX_HEADS:].astype(BF16)
        w_gt = jnp.zeros((BF16_SUBLANES, d), BF16).at[:FOX_HEADS].set(
            w_in[l, :, fox_cols:fox_cols + FOX_HEADS].T.astype(BF16))
        b_fp = jnp.zeros((BF16_SUBLANES, 1), F32).at[:FOX_HEADS, 0].set(b_f[l])
        per_head = lambda g, n: jnp.tile(g, n).reshape(1, n * HEAD_DIM)
        fq, fk, fv, kb, dq, dk, dv = _inproj(
            x, ln1_g[l].reshape(1, d), sh1, sc1, w_f, w_gt, w_d, b_fp,
            per_head(fox_qk_g[l, 0], FOX_HEADS), per_head(fox_qk_g[l, 1], FOX_HEADS),
            per_head(diff_qk_g[l, 0], 2 * DIFF_HEADS), per_head(diff_qk_g[l, 1], 2 * DIFF_HEADS))
        fox = _fox_attention(fq, fk, fv, kb)
        lam_init = 0.8 - 0.6 * math.exp(-0.3 * l)
        diff = _diff_attention(dq, dk, dv, diff_lam[l], diff_norm_g[l].reshape(1, 2 * HEAD_DIM), lam_init)
        x = _mlp(fox, diff, x, w_out[l].astype(BF16), g1, ln2_g[l].reshape(1, d), sh2, sc2, g2,
                 w_gate[l].astype(BF16), w_up[l].astype(BF16), w_down[l].astype(BF16))
    return x
```

```python
import functools
import math

import jax
import jax.numpy as jnp
from jax import lax
from jax.experimental import pallas as pl
from jax.experimental.pallas import tpu as pltpu

D_MODEL = 1024
HEAD_DIM = 64
FOX_HEADS = 8
DIFF_HEADS = 4
FOX_WIDTH = FOX_HEADS * HEAD_DIM
DIFF_WIDTH = DIFF_HEADS * 2 * HEAD_DIM
MOD_CHUNKS = 6
EPS = 1e-6

LANES = 128
BF16_SUBLANES = 16
LOG2E = math.log2(math.e)
QK_SCALE = LOG2E / math.sqrt(HEAD_DIM)
NEG = -1e30

TOKEN_TILE = 512
ATTN_TILE = 512
FFN_CHUNKS = 2
VMEM_LIMIT = 56 * 1024 * 1024

BF16 = jnp.bfloat16
F32 = jnp.float32

_NT = (((1,), (1,)), ((), ()))


def _dot(a, b):
    return jnp.dot(a, b, preferred_element_type=F32)


def _rms_modulate(x, gain, shift, scale):
    ms = jnp.mean(x * x, axis=-1, keepdims=True)
    return (x * lax.rsqrt(ms + EPS) * gain) * (1.0 + scale) + shift


def _ada_kernel(c_ref, w_ref, b_ref, o_ref):
    c = c_ref[...]
    cond = c / (1.0 + jnp.exp(-c))
    o_ref[...] = _dot(cond.astype(BF16), w_ref[...].astype(BF16)) + b_ref[...]


def _ada(c, w_ada, b_ada):
    depth, d, n = w_ada.shape
    batch = c.shape[0]
    nb = n // d
    return pl.pallas_call(
        _ada_kernel,
        grid=(depth, nb),
        in_specs=[
            pl.BlockSpec((batch, d), lambda l, j: (0, 0)),
            pl.BlockSpec((None, d, d), lambda l, j: (l, 0, j)),
            pl.BlockSpec((None, 1, d), lambda l, j: (l, 0, j)),
        ],
        out_specs=pl.BlockSpec((None, batch, d), lambda l, j: (l, 0, j)),
        out_shape=jax.ShapeDtypeStruct((depth, batch, n), F32),
        name="ada_mod",
    )(c, w_ada, b_ada.reshape(depth, 1, n))


def _head_rms(u, gain):
    rows, width = u.shape
    lane = lax.broadcasted_iota(jnp.int32, (rows, LANES), 1)
    lo = lane < HEAD_DIM
    outs = []
    for g in range(width // LANES):
        blk = u[:, g * LANES:(g + 1) * LANES]
        sq = blk * blk
        ss_lo = jnp.sum(jnp.where(lo, sq, 0.0), axis=-1, keepdims=True)
        ss_hi = jnp.sum(jnp.where(lo, 0.0, sq), axis=-1, keepdims=True)
        r = jnp.where(lo, lax.rsqrt(ss_lo / HEAD_DIM + EPS), lax.rsqrt(ss_hi / HEAD_DIM + EPS))
        outs.append(blk * r * gain[:, g * LANES:(g + 1) * LANES])
    return jnp.concatenate(outs, axis=-1)


def _inproj_kernel(x_ref, g_ref, sh_ref, sc_ref, wf_ref, wg_ref, wd_ref, bf_ref,
                   fqg_ref, fkg_ref, dqg_ref, dkg_ref,
                   fq_ref, fk_ref, fv_ref, kb_ref, dq_ref, dk_ref, dv_ref, carry_ref):
    si = pl.program_id(1)
    tm = x_ref.shape[0]
    h = _rms_modulate(x_ref[...], g_ref[...], sh_ref[...], sc_ref[...]).astype(BF16)

    w = FOX_WIDTH
    fq_ref[...] = _head_rms(_dot(h, wf_ref[:, 0:w]), fqg_ref[...] * QK_SCALE).astype(BF16)
    fk_ref[...] = _head_rms(_dot(h, wf_ref[:, w:2 * w]), fkg_ref[...]).astype(BF16)
    fv_ref[...] = _dot(h, wf_ref[:, 2 * w:3 * w]).astype(BF16)
    w = DIFF_WIDTH
    dq_ref[...] = _head_rms(_dot(h, wd_ref[:, 0:w]), dqg_ref[...] * QK_SCALE).astype(BF16)
    dk_ref[...] = _head_rms(_dot(h, wd_ref[:, w:2 * w]), dkg_ref[...]).astype(BF16)
    dv_ref[...] = _dot(h, wd_ref[:, 2 * w:3 * w]).astype(BF16)

    z = lax.dot_general(wg_ref[...], h, _NT, preferred_element_type=F32) + bf_ref[...]
    logf = jnp.minimum(z, 0.0) - jnp.log1p(jnp.exp(-jnp.abs(z)))

    hi = logf.astype(BF16)
    r1 = logf - hi.astype(F32)
    mid = r1.astype(BF16)
    low = (r1 - mid.astype(F32)).astype(BF16)
    src = lax.broadcasted_iota(jnp.int32, (tm, tm), 0)
    dst = lax.broadcasted_iota(jnp.int32, (tm, tm), 1)
    tri = jnp.where(src <= dst, 1.0, 0.0).astype(BF16)
    local = _dot(hi, tri) + _dot(mid, tri) + _dot(low, tri)

    @pl.when(si == 0)
    def _():
        carry_ref[...] = jnp.zeros_like(carry_ref)

    carry = carry_ref[:, 0:1]
    cum = carry + local
    kb_ref[...] = (-LOG2E) * cum[0:FOX_HEADS, :]
    carry_ref[...] = jnp.broadcast_to(carry + jnp.sum(logf, axis=-1, keepdims=True), carry_ref.shape)


def _inproj(x, gain, shift, scale, w_f, w_gt, w_d, b_f, fqg, fkg, dqg, dkg):
    batch, seq, d = x.shape
    tm = TOKEN_TILE
    tok = lambda width: pl.BlockSpec((None, tm, width), lambda b, s: (b, s, 0))
    vec = lambda width: pl.BlockSpec((None, 1, width), lambda b, s: (b, 0, 0))
    const = lambda shape: pl.BlockSpec(shape, lambda b, s: (0,) * len(shape),
                                       pipeline_mode=pl.Buffered(1))
    act = lambda width: jax.ShapeDtypeStruct((batch, seq, width), BF16)
    return pl.pallas_call(
        _inproj_kernel,
        grid=(batch, seq // tm),
        in_specs=[
            tok(d), const((1, d)), vec(d), vec(d),
            const(w_f.shape), const(w_gt.shape), const(w_d.shape), const(b_f.shape),
            const((1, FOX_WIDTH)), const((1, FOX_WIDTH)), const((1, DIFF_WIDTH)), const((1, DIFF_WIDTH)),
        ],
        out_specs=[
            tok(FOX_WIDTH), tok(FOX_WIDTH), tok(FOX_WIDTH),
            pl.BlockSpec((None, FOX_HEADS, tm), lambda b, s: (b, 0, s)),
            tok(DIFF_WIDTH), tok(DIFF_WIDTH), tok(DIFF_WIDTH),
        ],
        out_shape=[
            act(FOX_WIDTH), act(FOX_WIDTH), act(FOX_WIDTH),
            jax.ShapeDtypeStruct((batch, FOX_HEADS, seq), F32),
            act(DIFF_WIDTH), act(DIFF_WIDTH), act(DIFF_WIDTH),
        ],
        scratch_shapes=[pltpu.VMEM((BF16_SUBLANES, LANES), F32)],
        compiler_params=pltpu.CompilerParams(
            dimension_semantics=("arbitrary", "arbitrary"), vmem_limit_bytes=VMEM_LIMIT),
        name="inproj",
    )(x, gain, shift, scale, w_f, w_gt, w_d, b_f, fqg, fkg, dqg, dkg)


def _causal(s):
    row = lax.broadcasted_iota(jnp.int32, s.shape, 0)
    col = lax.broadcasted_iota(jnp.int32, s.shape, 1)
    return jnp.where(col <= row, s, NEG)


def _split_halves(q):
    lane = lax.broadcasted_iota(jnp.int32, q.shape, 1)
    lo = lane < HEAD_DIM
    zero = jnp.zeros_like(q)
    return lo, jnp.where(lo, q, zero), jnp.where(lo, zero, q)


def _softmax_update(s2, m2, l2):
    ps, alphas, ms, ls = [], [], [], []
    for s, m, l in zip(s2, m2, l2):
        m_new = jnp.maximum(m, jnp.max(s, axis=-1, keepdims=True))
        p = jnp.exp2(s - m_new)
        alpha = jnp.exp2(m - m_new)
        part = p[:, 0:LANES]
        for c in range(1, s.shape[1] // LANES):
            part = part + p[:, c * LANES:(c + 1) * LANES]
        ps.append(p.astype(BF16))
        alphas.append(alpha)
        ms.append(m_new)
        ls.append(alpha * l + part)
    return tuple(ps), tuple(alphas), tuple(ms), tuple(ls)


def _flash(qs, k_ref, v_ref, bias_fn, accumulate, acc0, qi, t):
    neg = jnp.full((t, 1), NEG, F32)
    zero = jnp.zeros((t, LANES), F32)
    m2, l2, acc = (neg, neg), (zero, zero), acc0
    for j in range(qi + 1):
        rows = slice(j * t, (j + 1) * t)
        k = k_ref[rows, :]
        s2 = []
        for q, bias in zip(qs, bias_fn(j * t)):
            s = lax.dot_general(q, k, _NT, preferred_element_type=F32) + bias
            s2.append(_causal(s) if j == qi else s)
        p2, alpha2, m2, l2 = _softmax_update(s2, m2, l2)
        acc = accumulate(acc, alpha2, p2, v_ref[rows, :])
    return acc, tuple(jnp.sum(l, axis=-1, keepdims=True) for l in l2)


def _fox_kernel(q_ref, k_ref, v_ref, kb_ref, o_ref):
    t = ATTN_TILE
    for qi in range(q_ref.shape[0] // t):
        rows = slice(qi * t, (qi + 1) * t)
        lo, q0, q1 = _split_halves(q_ref[rows, :])

        def bias_fn(start):
            return kb_ref[0:1, start:start + t], kb_ref[1:2, start:start + t]

        def accumulate(acc, alpha2, p2, v):
            pv = jnp.where(lo, _dot(p2[0], v), _dot(p2[1], v))
            return acc * jnp.where(lo, alpha2[0], alpha2[1]) + pv

        acc, l2 = _flash((q0, q1), k_ref, v_ref, bias_fn, accumulate, jnp.zeros((t, LANES), F32), qi, t)
        o_ref[rows, :] = (acc / jnp.where(lo, l2[0], l2[1])).astype(o_ref.dtype)


def _fox_attention(q, k, v, kb):
    batch, seq, width = q.shape
    pairs = width // LANES
    kb = kb.reshape(batch, pairs, 2, seq)
    seq_block = pl.BlockSpec((None, seq, LANES), lambda b, h: (b, 0, h))
    return pl.pallas_call(
        _fox_kernel,
        grid=(batch, pairs),
        in_specs=[seq_block, seq_block, seq_block,
                  pl.BlockSpec((None, None, 2, seq), lambda b, h: (b, h, 0, 0))],
        out_specs=seq_block,
        out_shape=jax.ShapeDtypeStruct((batch, seq, width), BF16),
        compiler_params=pltpu.CompilerParams(dimension_semantics=("arbitrary", "arbitrary")),
        name="fox_attn",
    )(q, k, v, kb)


def _diff_kernel(q_ref, k_ref, v_ref, lam_ref, ng_ref, o_ref, *, lam_init):
    head = pl.program_id(1)
    t = ATTN_TILE
    slope = jnp.exp2(-2.0 * (jnp.full((1, 1), head, jnp.int32) + 1).astype(F32))
    lv = lam_ref[...]
    lam = (jnp.exp(jnp.sum(lv[0:1] * lv[1:2], axis=-1, keepdims=True))
           - jnp.exp(jnp.sum(lv[2:3] * lv[3:4], axis=-1, keepdims=True)) + lam_init)

    def bias_fn(start):
        pos = (start + lax.broadcasted_iota(jnp.int32, (1, t), 1)).astype(F32)
        kb = (LOG2E * slope) * pos
        return kb, kb

    def accumulate(acc, alpha2, p2, v):
        return tuple(a * alpha + _dot(p, v) for a, alpha, p in zip(acc, alpha2, p2))

    for qi in range(q_ref.shape[0] // t):
        rows = slice(qi * t, (qi + 1) * t)
        _, q1, q2 = _split_halves(q_ref[rows, :])
        acc, l = _flash((q1, q2), k_ref, v_ref, bias_fn, accumulate, (jnp.zeros((t, LANES), F32),) * 2, qi, t)
        o = acc[0] / l[0] - lam * (acc[1] / l[1])
        ms = jnp.mean(o * o, axis=-1, keepdims=True)
        o_ref[rows, :] = ((o * lax.rsqrt(ms + EPS) * ng_ref[...]) * (1.0 - lam_init)).astype(o_ref.dtype)


def _diff_attention(q, k, v, lam_params, norm_gain, lam_init):
    batch, seq, width = q.shape
    heads = width // LANES
    seq_block = pl.BlockSpec((None, seq, LANES), lambda b, h: (b, 0, h))
    return pl.pallas_call(
        functools.partial(_diff_kernel, lam_init=lam_init),
        grid=(batch, heads),
        in_specs=[seq_block, seq_block, seq_block,
                  pl.BlockSpec(lam_params.shape, lambda b, h: (0, 0)),
                  pl.BlockSpec(norm_gain.shape, lambda b, h: (0, 0))],
        out_specs=seq_block,
        out_shape=jax.ShapeDtypeStruct((batch, seq, width), BF16),
        compiler_params=pltpu.CompilerParams(dimension_semantics=("arbitrary", "arbitrary")),
        name="diff_attn",
    )(q, k, v, lam_params, norm_gain)


def _mlp_kernel(fox_ref, diff_ref, x_ref, wo_ref, g1_ref, ln_ref, sh_ref, sc_ref, g2_ref,
                wg_ref, wu_ref, wd_ref, o_ref):
    mixed = _dot(fox_ref[...], wo_ref[0:FOX_WIDTH, :]) + _dot(diff_ref[...], wo_ref[FOX_WIDTH:, :])
    x1 = x_ref[...] + g1_ref[...] * mixed
    h = _rms_modulate(x1, ln_ref[...], sh_ref[...], sc_ref[...]).astype(BF16)
    d_ff = wg_ref.shape[1]
    chunk = d_ff // FFN_CHUNKS
    y = None
    for c in range(FFN_CHUNKS):
        cols = slice(c * chunk, (c + 1) * chunk)
        gate = _dot(h, wg_ref[:, cols])
        up = _dot(h, wu_ref[:, cols])
        a = ((gate / (1.0 + jnp.exp(-gate))) * up).astype(BF16)
        part = _dot(a, wd_ref[cols, :])
        y = part if y is None else y + part
    o_ref[...] = x1 + g2_ref[...] * y


def _mlp(fox, diff, x, w_o, g1, ln_g, shift, scale, g2, w_gate, w_up, w_down):
    batch, seq, d = x.shape
    tm = TOKEN_TILE
    tok = lambda width: pl.BlockSpec((None, tm, width), lambda b, s: (b, s, 0))
    vec = lambda width: pl.BlockSpec((None, 1, width), lambda b, s: (b, 0, 0))
    const = lambda shape: pl.BlockSpec(shape, lambda b, s: (0,) * len(shape),
                                       pipeline_mode=pl.Buffered(1))
    return pl.pallas_call(
        _mlp_kernel,
        grid=(batch, seq // tm),
        in_specs=[
            tok(FOX_WIDTH), tok(DIFF_WIDTH), tok(d), const(w_o.shape), vec(d), const((1, d)),
            vec(d), vec(d), vec(d), const(w_gate.shape), const(w_up.shape), const(w_down.shape),
        ],
        out_specs=tok(d),
        out_shape=jax.ShapeDtypeStruct((batch, seq, d), F32),
        compiler_params=pltpu.CompilerParams(
            dimension_semantics=("arbitrary", "arbitrary"), vmem_limit_bytes=VMEM_LIMIT),
        name="out_mlp",
    )(fox, diff, x, w_o, g1, ln_g, shift, scale, g2, w_gate, w_up, w_down)


def kernel(x, c, ln1_g, ln2_g, w_ada, b_ada, w_in, b_f, fox_qk_g, diff_qk_g, diff_lam, diff_norm_g,
           w_out, w_gate, w_up, w_down):
    depth = w_in.shape[0]
    batch, _, d = x.shape
    mod = _ada(c, w_ada, b_ada)
    fox_cols = 3 * FOX_WIDTH
    for l in range(depth):
        sh1, sc1, g1, sh2, sc2, g2 = (
            mod[l, :, i * d:(i + 1) * d].reshape(batch, 1, d) for i in range(MOD_CHUNKS))
        w_f = w_in[l, :, :fox_cols].astype(BF16)
        w_d = w_in[l, :, fox_cols + FOX_HEADS:].astype(BF16)
        w_gt = jnp.zeros((BF16_SUBLANES, d), BF16).at[:FOX_HEADS].set(
            w_in[l, :, fox_cols:fox_cols + FOX_HEADS].T.astype(BF16))
        b_fp = jnp.zeros((BF16_SUBLANES, 1), F32).at[:FOX_HEADS, 0].set(b_f[l])
        per_head = lambda g, n: jnp.tile(g, n).reshape(1, n * HEAD_DIM)
        fq, fk, fv, kb, dq, dk, dv = _inproj(
            x, ln1_g[l].reshape(1, d), sh1, sc1, w_f, w_gt, w_d, b_fp,
            per_head(fox_qk_g[l, 0], FOX_HEADS), per_head(fox_qk_g[l, 1], FOX_HEADS),
            per_head(diff_qk_g[l, 0], 2 * DIFF_HEADS), per_head(diff_qk_g[l, 1], 2 * DIFF_HEADS))
        fox = _fox_attention(fq, fk, fv, kb)
        lam_init = 0.8 - 0.6 * math.exp(-0.3 * l)
        diff = _diff_attention(dq, dk, dv, diff_lam[l], diff_norm_g[l].reshape(1, 2 * HEAD_DIM), lam_init)
        x = _mlp(fox, diff, x, w_out[l].astype(BF16), g1, ln2_g[l].reshape(1, d), sh2, sc2, g2,
                 w_gate[l].astype(BF16), w_up[l].astype(BF16), w_down[l].astype(BF16))
    return x
```

```python
import functools
import math

import jax
import jax.numpy as jnp
from jax import lax
from jax.experimental import pallas as pl
from jax.experimental.pallas import tpu as pltpu

D_MODEL = 1024
HEAD_DIM = 64
FOX_HEADS = 8
DIFF_HEADS = 4
FOX_WIDTH = FOX_HEADS * HEAD_DIM
DIFF_WIDTH = DIFF_HEADS * 2 * HEAD_DIM
MOD_CHUNKS = 6
EPS = 1e-6

LANES = 128
MXU_WIDTH = 256
BF16_SUBLANES = 16
LOG2E = math.log2(math.e)
QK_SCALE = LOG2E / math.sqrt(HEAD_DIM)
NEG = -1e30

TOKEN_TILE = 512
ATTN_TILE = 512
FFN_CHUNKS = 2
VMEM_LIMIT = 56 * 1024 * 1024

BF16 = jnp.bfloat16
F32 = jnp.float32

_NT = (((1,), (1,)), ((), ()))


def _dot(a, b):
    return jnp.dot(a, b, preferred_element_type=F32)


def _rms_modulate(x, gain, shift, scale):
    ms = jnp.mean(x * x, axis=-1, keepdims=True)
    return (x * lax.rsqrt(ms + EPS) * gain) * (1.0 + scale) + shift


def _ada_kernel(c_ref, w_ref, b_ref, o_ref):
    c = c_ref[...]
    cond = c / (1.0 + jnp.exp(-c))
    o_ref[...] = _dot(cond.astype(BF16), w_ref[...].astype(BF16)) + b_ref[...]


def _ada(c, w_ada, b_ada):
    depth, d, n = w_ada.shape
    batch = c.shape[0]
    nb = n // d
    return pl.pallas_call(
        _ada_kernel,
        grid=(depth, nb),
        in_specs=[
            pl.BlockSpec((batch, d), lambda l, j: (0, 0)),
            pl.BlockSpec((None, d, d), lambda l, j: (l, 0, j)),
            pl.BlockSpec((None, 1, d), lambda l, j: (l, 0, j)),
        ],
        out_specs=pl.BlockSpec((None, batch, d), lambda l, j: (l, 0, j)),
        out_shape=jax.ShapeDtypeStruct((depth, batch, n), F32),
        name="ada_mod",
    )(c, w_ada, b_ada.reshape(depth, 1, n))


def _head_rms(u, gain):
    rows, width = u.shape
    lane = lax.broadcasted_iota(jnp.int32, (rows, LANES), 1)
    lo = lane < HEAD_DIM
    outs = []
    for g in range(width // LANES):
        blk = u[:, g * LANES:(g + 1) * LANES]
        sq = blk * blk
        ss_lo = jnp.sum(jnp.where(lo, sq, 0.0), axis=-1, keepdims=True)
        ss_hi = jnp.sum(jnp.where(lo, 0.0, sq), axis=-1, keepdims=True)
        r = jnp.where(lo, lax.rsqrt(ss_lo / HEAD_DIM + EPS), lax.rsqrt(ss_hi / HEAD_DIM + EPS))
        outs.append(blk * r * gain[:, g * LANES:(g + 1) * LANES])
    return jnp.concatenate(outs, axis=-1)


def _inproj_kernel(x_ref, g_ref, sh_ref, sc_ref, wf_ref, wg_ref, wd_ref, bf_ref,
                   fqg_ref, fkg_ref, dqg_ref, dkg_ref,
                   fq_ref, fk_ref, fv_ref, kb_ref, dq_ref, dk_ref, dv_ref, carry_ref):
    si = pl.program_id(1)
    tm = x_ref.shape[0]
    h = _rms_modulate(x_ref[...], g_ref[...], sh_ref[...], sc_ref[...]).astype(BF16)

    w = FOX_WIDTH
    fq_ref[...] = _head_rms(_dot(h, wf_ref[:, 0:w]), fqg_ref[...] * QK_SCALE).astype(BF16)
    fk_ref[...] = _head_rms(_dot(h, wf_ref[:, w:2 * w]), fkg_ref[...]).astype(BF16)
    fv_ref[...] = _dot(h, wf_ref[:, 2 * w:3 * w]).astype(BF16)
    w = DIFF_WIDTH
    dq_ref[...] = _head_rms(_dot(h, wd_ref[:, 0:w]), dqg_ref[...] * QK_SCALE).astype(BF16)
    dk_ref[...] = _head_rms(_dot(h, wd_ref[:, w:2 * w]), dkg_ref[...]).astype(BF16)
    dv_ref[...] = _dot(h, wd_ref[:, 2 * w:3 * w]).astype(BF16)

    z = lax.dot_general(wg_ref[...], h, _NT, preferred_element_type=F32) + bf_ref[...]
    logf = jnp.minimum(z, 0.0) - jnp.log1p(jnp.exp(-jnp.abs(z)))

    hi = logf.astype(BF16)
    r1 = logf - hi.astype(F32)
    mid = r1.astype(BF16)
    low = (r1 - mid.astype(F32)).astype(BF16)
    src = lax.broadcasted_iota(jnp.int32, (tm, tm), 0)
    dst = lax.broadcasted_iota(jnp.int32, (tm, tm), 1)
    tri = jnp.where(src <= dst, 1.0, 0.0).astype(BF16)
    local = _dot(hi, tri) + _dot(mid, tri) + _dot(low, tri)

    @pl.when(si == 0)
    def _():
        carry_ref[...] = jnp.zeros_like(carry_ref)

    carry = carry_ref[:, 0:1]
    cum = carry + local
    kb_ref[...] = (-LOG2E) * cum[0:FOX_HEADS, :]
    carry_ref[...] = jnp.broadcast_to(carry + jnp.sum(logf, axis=-1, keepdims=True), carry_ref.shape)


def _inproj(x, gain, shift, scale, w_f, w_gt, w_d, b_f, fqg, fkg, dqg, dkg):
    batch, seq, d = x.shape
    tm = TOKEN_TILE
    tok = lambda width: pl.BlockSpec((None, tm, width), lambda b, s: (b, s, 0))
    vec = lambda width: pl.BlockSpec((None, 1, width), lambda b, s: (b, 0, 0))
    const = lambda shape: pl.BlockSpec(shape, lambda b, s: (0,) * len(shape),
                                       pipeline_mode=pl.Buffered(1))
    act = lambda width: jax.ShapeDtypeStruct((batch, seq, width), BF16)
    return pl.pallas_call(
        _inproj_kernel,
        grid=(batch, seq // tm),
        in_specs=[
            tok(d), const((1, d)), vec(d), vec(d),
            const(w_f.shape), const(w_gt.shape), const(w_d.shape), const(b_f.shape),
            const((1, FOX_WIDTH)), const((1, FOX_WIDTH)), const((1, DIFF_WIDTH)), const((1, DIFF_WIDTH)),
        ],
        out_specs=[
            tok(FOX_WIDTH), tok(FOX_WIDTH), tok(FOX_WIDTH),
            pl.BlockSpec((None, FOX_HEADS, tm), lambda b, s: (b, 0, s)),
            tok(DIFF_WIDTH), tok(DIFF_WIDTH), tok(DIFF_WIDTH),
        ],
        out_shape=[
            act(FOX_WIDTH), act(FOX_WIDTH), act(FOX_WIDTH),
            jax.ShapeDtypeStruct((batch, FOX_HEADS, seq), F32),
            act(DIFF_WIDTH), act(DIFF_WIDTH), act(DIFF_WIDTH),
        ],
        scratch_shapes=[pltpu.VMEM((BF16_SUBLANES, LANES), F32)],
        compiler_params=pltpu.CompilerParams(
            dimension_semantics=("arbitrary", "arbitrary"), vmem_limit_bytes=VMEM_LIMIT),
        name="inproj",
    )(x, gain, shift, scale, w_f, w_gt, w_d, b_f, fqg, fkg, dqg, dkg)


def _causal(s):
    row = lax.broadcasted_iota(jnp.int32, s.shape, 0)
    col = lax.broadcasted_iota(jnp.int32, s.shape, 1)
    return jnp.where(col <= row, s, NEG)


def _split_halves(q):
    lane = lax.broadcasted_iota(jnp.int32, q.shape, 1)
    lo = lane < HEAD_DIM
    zero = jnp.zeros_like(q)
    return lo, jnp.where(lo, q, zero), jnp.where(lo, zero, q)


def _softmax_update(s2, m2, l2):
    ps, alphas, ms, ls = [], [], [], []
    for s, m, l in zip(s2, m2, l2):
        m_new = jnp.maximum(m, jnp.max(s, axis=-1, keepdims=True))
        p = jnp.exp2(s - m_new)
        alpha = jnp.exp2(m - m_new)
        part = p[:, 0:LANES]
        for c in range(1, s.shape[1] // LANES):
            part = part + p[:, c * LANES:(c + 1) * LANES]
        ps.append(p.astype(BF16))
        alphas.append(alpha)
        ms.append(m_new)
        ls.append(alpha * l + part)
    return tuple(ps), tuple(alphas), tuple(ms), tuple(ls)


def _flash(qs, k_ref, v_ref, bias_fn, accumulate, acc0, qi, t):
    neg = jnp.full((t, 1), NEG, F32)
    zero = jnp.zeros((t, LANES), F32)
    m2, l2, acc = (neg, neg), (zero, zero), acc0
    for j in range(qi + 1):
        rows = slice(j * t, (j + 1) * t)
        k = k_ref[rows, :]
        s2 = []
        for q, bias in zip(qs, bias_fn(j * t)):
            s = lax.dot_general(q, k, _NT, preferred_element_type=F32) + bias
            s2.append(_causal(s) if j == qi else s)
        p2, alpha2, m2, l2 = _softmax_update(s2, m2, l2)
        acc = accumulate(acc, alpha2, p2, v_ref[rows, :])
    return acc, tuple(jnp.sum(l, axis=-1, keepdims=True) for l in l2)


def _fox_kernel(q_ref, k_ref, v_ref, kb_ref, o_ref):
    t = ATTN_TILE
    for qi in range(q_ref.shape[0] // t):
        rows = slice(qi * t, (qi + 1) * t)
        lo, q0, q1 = _split_halves(q_ref[rows, :])

        def bias_fn(start):
            return kb_ref[0:1, start:start + t], kb_ref[1:2, start:start + t]

        def accumulate(acc, alpha2, p2, v):
            pv = jnp.where(lo, _dot(p2[0], v), _dot(p2[1], v))
            return acc * jnp.where(lo, alpha2[0], alpha2[1]) + pv

        acc, l2 = _flash((q0, q1), k_ref, v_ref, bias_fn, accumulate, jnp.zeros((t, LANES), F32), qi, t)
        o_ref[rows, :] = (acc / jnp.where(lo, l2[0], l2[1])).astype(o_ref.dtype)


def _diff_kernel(q_ref, k_ref, v_ref, lam_ref, ng_ref, o_ref, *, lam_init):
    head = pl.program_id(1)
    t = ATTN_TILE
    slope = jnp.exp2(-2.0 * (jnp.full((1, 1), head, jnp.int32) + 1).astype(F32))
    lv = lam_ref[...]
    lam = (jnp.exp(jnp.sum(lv[0:1] * lv[1:2], axis=-1, keepdims=True))
           - jnp.exp(jnp.sum(lv[2:3] * lv[3:4], axis=-1, keepdims=True)) + lam_init)

    def bias_fn(start):
        pos = (start + lax.broadcasted_iota(jnp.int32, (1, t), 1)).astype(F32)
        kb = (LOG2E * slope) * pos
        return kb, kb

    def accumulate(acc, alpha2, p2, v):
        return tuple(a * alpha + _dot(p, v) for a, alpha, p in zip(acc, alpha2, p2))

    for qi in range(q_ref.shape[0] // t):
        rows = slice(qi * t, (qi + 1) * t)
        _, q1, q2 = _split_halves(q_ref[rows, :])
        acc, l = _flash((q1, q2), k_ref, v_ref, bias_fn, accumulate, (jnp.zeros((t, LANES), F32),) * 2, qi, t)
        o = acc[0] / l[0] - lam * (acc[1] / l[1])
        ms = jnp.mean(o * o, axis=-1, keepdims=True)
        o_ref[rows, :] = ((o * lax.rsqrt(ms + EPS) * ng_ref[...]) * (1.0 - lam_init)).astype(o_ref.dtype)


def _attn_kernel(fq_ref, fk_ref, fv_ref, kb_ref, dq_ref, dk_ref, dv_ref, lam_ref, ng_ref,
                 fox_ref, diff_ref, *, lam_init):
    _fox_kernel(fq_ref, fk_ref, fv_ref, kb_ref, fox_ref)
    _diff_kernel(dq_ref, dk_ref, dv_ref, lam_ref, ng_ref, diff_ref, lam_init=lam_init)


def _attention(fq, fk, fv, kb, dq, dk, dv, lam_params, norm_gain, lam_init):
    batch, seq, width = fq.shape
    groups = width // LANES
    assert dq.shape[2] // LANES == groups
    kb = kb.reshape(batch, groups, 2, seq)
    seq_block = pl.BlockSpec((None, seq, LANES), lambda b, h: (b, 0, h))
    out = jax.ShapeDtypeStruct((batch, seq, width), BF16)
    return pl.pallas_call(
        functools.partial(_attn_kernel, lam_init=lam_init),
        grid=(batch, groups),
        in_specs=[seq_block, seq_block, seq_block,
                  pl.BlockSpec((None, None, 2, seq), lambda b, h: (b, h, 0, 0)),
                  seq_block, seq_block, seq_block,
                  pl.BlockSpec(lam_params.shape, lambda b, h: (0, 0)),
                  pl.BlockSpec(norm_gain.shape, lambda b, h: (0, 0))],
        out_specs=[seq_block, seq_block],
        out_shape=[out, out],
        compiler_params=pltpu.CompilerParams(dimension_semantics=("arbitrary", "arbitrary")),
        name="attention",
    )(fq, fk, fv, kb, dq, dk, dv, lam_params, norm_gain)


def _mlp_kernel(fox_ref, diff_ref, x_ref, wo_ref, g1_ref, ln_ref, sh_ref, sc_ref, g2_ref,
                wg_ref, wu_ref, wd_ref, o_ref):
    mixed = _dot(fox_ref[...], wo_ref[0:FOX_WIDTH, :]) + _dot(diff_ref[...], wo_ref[FOX_WIDTH:, :])
    x1 = x_ref[...] + g1_ref[...] * mixed
    h = _rms_modulate(x1, ln_ref[...], sh_ref[...], sc_ref[...]).astype(BF16)
    d_ff = wg_ref.shape[1]
    tiles = d_ff // MXU_WIDTH
    bounds = [(tiles * c // FFN_CHUNKS) * MXU_WIDTH for c in range(FFN_CHUNKS)] + [d_ff]
    y = None
    for c in range(FFN_CHUNKS):
        cols = slice(bounds[c], bounds[c + 1])
        gate = _dot(h, wg_ref[:, cols])
        up = _dot(h, wu_ref[:, cols])
        a = ((gate / (1.0 + jnp.exp(-gate))) * up).astype(BF16)
        part = _dot(a, wd_ref[cols, :])
        y = part if y is None else y + part
    o_ref[...] = x1 + g2_ref[...] * y


def _mlp(fox, diff, x, w_o, g1, ln_g, shift, scale, g2, w_gate, w_up, w_down):
    batch, seq, d = x.shape
    tm = TOKEN_TILE
    tok = lambda width: pl.BlockSpec((None, tm, width), lambda b, s: (b, s, 0))
    vec = lambda width: pl.BlockSpec((None, 1, width), lambda b, s: (b, 0, 0))
    const = lambda shape: pl.BlockSpec(shape, lambda b, s: (0,) * len(shape),
                                       pipeline_mode=pl.Buffered(1))
    return pl.pallas_call(
        _mlp_kernel,
        grid=(batch, seq // tm),
        in_specs=[
            tok(FOX_WIDTH), tok(DIFF_WIDTH), tok(d), const(w_o.shape), vec(d), const((1, d)),
            vec(d), vec(d), vec(d), const(w_gate.shape), const(w_up.shape), const(w_down.shape),
        ],
        out_specs=tok(d),
        out_shape=jax.ShapeDtypeStruct((batch, seq, d), F32),
        compiler_params=pltpu.CompilerParams(
            dimension_semantics=("arbitrary", "arbitrary"), vmem_limit_bytes=VMEM_LIMIT),
        name="out_mlp",
    )(fox, diff, x, w_o, g1, ln_g, shift, scale, g2, w_gate, w_up, w_down)


def kernel(x, c, ln1_g, ln2_g, w_ada, b_ada, w_in, b_f, fox_qk_g, diff_qk_g, diff_lam, diff_norm_g,
           w_out, w_gate, w_up, w_down):
    depth = w_in.shape[0]
    batch, _, d = x.shape
    mod = _ada(c, w_ada, b_ada)
    fox_cols = 3 * FOX_WIDTH
    for l in range(depth):
        sh1, sc1, g1, sh2, sc2, g2 = (
            mod[l, :, i * d:(i + 1) * d].reshape(batch, 1, d) for i in range(MOD_CHUNKS))
        w_f = w_in[l, :, :fox_cols].astype(BF16)
        w_d = w_in[l, :, fox_cols + FOX_HEADS:].astype(BF16)
        w_gt = jnp.zeros((BF16_SUBLANES, d), BF16).at[:FOX_HEADS].set(
            w_in[l, :, fox_cols:fox_cols + FOX_HEADS].T.astype(BF16))
        b_fp = jnp.zeros((BF16_SUBLANES, 1), F32).at[:FOX_HEADS, 0].set(b_f[l])
        per_head = lambda g, n: jnp.tile(g, n).reshape(1, n * HEAD_DIM)
        fq, fk, fv, kb, dq, dk, dv = _inproj(
            x, ln1_g[l].reshape(1, d), sh1, sc1, w_f, w_gt, w_d, b_fp,
            per_head(fox_qk_g[l, 0], FOX_HEADS), per_head(fox_qk_g[l, 1], FOX_HEADS),
            per_head(diff_qk_g[l, 0], 2 * DIFF_HEADS), per_head(diff_qk_g[l, 1], 2 * DIFF_HEADS))
        lam_init = 0.8 - 0.6 * math.exp(-0.3 * l)
        fox, diff = _attention(fq, fk, fv, kb, dq, dk, dv, diff_lam[l],
                               diff_norm_g[l].reshape(1, 2 * HEAD_DIM), lam_init)
        x = _mlp(fox, diff, x, w_out[l].astype(BF16), g1, ln2_g[l].reshape(1, d), sh2, sc2, g2,
                 w_gate[l].astype(BF16), w_up[l].astype(BF16), w_down[l].astype(BF16))
    return x
```

```python
import functools
import math

import jax
import jax.numpy as jnp
from jax import lax
from jax.experimental import pallas as pl
from jax.experimental.pallas import tpu as pltpu

D_MODEL = 1024
HEAD_DIM = 64
FOX_HEADS = 8
DIFF_HEADS = 4
FOX_WIDTH = FOX_HEADS * HEAD_DIM
DIFF_WIDTH = DIFF_HEADS * 2 * HEAD_DIM
MOD_CHUNKS = 6
EPS = 1e-6

LANES = 128
MXU_WIDTH = 256
BF16_SUBLANES = 16
LOG2E = math.log2(math.e)
QK_SCALE = LOG2E / math.sqrt(HEAD_DIM)
NEG = -1e30

TOKEN_TILE = 512
ATTN_TILE = 512
KEY_TILE = 512
FFN_CHUNKS = 2
VMEM_LIMIT = 56 * 1024 * 1024

BF16 = jnp.bfloat16
F32 = jnp.float32

_NT = (((1,), (1,)), ((), ()))


def _dot(a, b):
    return jnp.dot(a, b, preferred_element_type=F32)


def _rms_modulate(x, gain, shift, scale):
    ms = jnp.mean(x * x, axis=-1, keepdims=True)
    return (x * lax.rsqrt(ms + EPS) * gain) * (1.0 + scale) + shift


def _ada_kernel(c_ref, w_ref, b_ref, o_ref):
    c = c_ref[...]
    cond = c / (1.0 + jnp.exp(-c))
    o_ref[...] = _dot(cond.astype(BF16), w_ref[...].astype(BF16)) + b_ref[...]


def _ada(c, w_ada, b_ada):
    depth, d, n = w_ada.shape
    batch = c.shape[0]
    nb = n // d
    return pl.pallas_call(
        _ada_kernel,
        grid=(depth, nb),
        in_specs=[
            pl.BlockSpec((batch, d), lambda l, j: (0, 0)),
            pl.BlockSpec((None, d, d), lambda l, j: (l, 0, j)),
            pl.BlockSpec((None, 1, d), lambda l, j: (l, 0, j)),
        ],
        out_specs=pl.BlockSpec((None, batch, d), lambda l, j: (l, 0, j)),
        out_shape=jax.ShapeDtypeStruct((depth, batch, n), F32),
        name="ada_mod",
    )(c, w_ada, b_ada.reshape(depth, 1, n))


def _head_rms(u, gain):
    rows, width = u.shape
    lane = lax.broadcasted_iota(jnp.int32, (rows, LANES), 1)
    lo = lane < HEAD_DIM
    outs = []
    for g in range(width // LANES):
        blk = u[:, g * LANES:(g + 1) * LANES]
        sq = blk * blk
        ss_lo = jnp.sum(jnp.where(lo, sq, 0.0), axis=-1, keepdims=True)
        ss_hi = jnp.sum(jnp.where(lo, 0.0, sq), axis=-1, keepdims=True)
        r = jnp.where(lo, lax.rsqrt(ss_lo / HEAD_DIM + EPS), lax.rsqrt(ss_hi / HEAD_DIM + EPS))
        outs.append(blk * r * gain[:, g * LANES:(g + 1) * LANES])
    return jnp.concatenate(outs, axis=-1)


def _inproj_kernel(x_ref, g_ref, sh_ref, sc_ref, wf_ref, wg_ref, wd_ref, bf_ref,
                   fqg_ref, fkg_ref, dqg_ref, dkg_ref,
                   fq_ref, fk_ref, fv_ref, kb_ref, dq_ref, dk_ref, dv_ref, carry_ref):
    si = pl.program_id(1)
    tm = x_ref.shape[0]
    h = _rms_modulate(x_ref[...], g_ref[...], sh_ref[...], sc_ref[...]).astype(BF16)

    w = FOX_WIDTH
    fq_ref[...] = _head_rms(_dot(h, wf_ref[:, 0:w]), fqg_ref[...] * QK_SCALE).astype(BF16)
    fk_ref[...] = _head_rms(_dot(h, wf_ref[:, w:2 * w]), fkg_ref[...]).astype(BF16)
    fv_ref[...] = _dot(h, wf_ref[:, 2 * w:3 * w]).astype(BF16)
    w = DIFF_WIDTH
    dq_ref[...] = _head_rms(_dot(h, wd_ref[:, 0:w]), dqg_ref[...] * QK_SCALE).astype(BF16)
    dk_ref[...] = _head_rms(_dot(h, wd_ref[:, w:2 * w]), dkg_ref[...]).astype(BF16)
    dv_ref[...] = _dot(h, wd_ref[:, 2 * w:3 * w]).astype(BF16)

    z = lax.dot_general(wg_ref[...], h, _NT, preferred_element_type=F32) + bf_ref[...]
    logf = jnp.minimum(z, 0.0) - jnp.log1p(jnp.exp(-jnp.abs(z)))

    hi = logf.astype(BF16)
    r1 = logf - hi.astype(F32)
    mid = r1.astype(BF16)
    low = (r1 - mid.astype(F32)).astype(BF16)
    src = lax.broadcasted_iota(jnp.int32, (tm, tm), 0)
    dst = lax.broadcasted_iota(jnp.int32, (tm, tm), 1)
    tri = jnp.where(src <= dst, 1.0, 0.0).astype(BF16)
    local = _dot(hi, tri) + _dot(mid, tri) + _dot(low, tri)

    @pl.when(si == 0)
    def _():
        carry_ref[...] = jnp.zeros_like(carry_ref)

    carry = carry_ref[:, 0:1]
    cum = carry + local
    kb_ref[...] = (-LOG2E) * cum[0:FOX_HEADS, :]
    carry_ref[...] = jnp.broadcast_to(carry + jnp.sum(logf, axis=-1, keepdims=True), carry_ref.shape)


def _inproj(x, gain, shift, scale, w_f, w_gt, w_d, b_f, fqg, fkg, dqg, dkg):
    batch, seq, d = x.shape
    tm = TOKEN_TILE
    tok = lambda width: pl.BlockSpec((None, tm, width), lambda b, s: (b, s, 0))
    vec = lambda width: pl.BlockSpec((None, 1, width), lambda b, s: (b, 0, 0))
    const = lambda shape: pl.BlockSpec(shape, lambda b, s: (0,) * len(shape),
                                       pipeline_mode=pl.Buffered(1))
    act = lambda width: jax.ShapeDtypeStruct((batch, seq, width), BF16)
    return pl.pallas_call(
        _inproj_kernel,
        grid=(batch, seq // tm),
        in_specs=[
            tok(d), const((1, d)), vec(d), vec(d),
            const(w_f.shape), const(w_gt.shape), const(w_d.shape), const(b_f.shape),
            const((1, FOX_WIDTH)), const((1, FOX_WIDTH)), const((1, DIFF_WIDTH)), const((1, DIFF_WIDTH)),
        ],
        out_specs=[
            tok(FOX_WIDTH), tok(FOX_WIDTH), tok(FOX_WIDTH),
            pl.BlockSpec((None, FOX_HEADS, tm), lambda b, s: (b, 0, s)),
            tok(DIFF_WIDTH), tok(DIFF_WIDTH), tok(DIFF_WIDTH),
        ],
        out_shape=[
            act(FOX_WIDTH), act(FOX_WIDTH), act(FOX_WIDTH),
            jax.ShapeDtypeStruct((batch, FOX_HEADS, seq), F32),
            act(DIFF_WIDTH), act(DIFF_WIDTH), act(DIFF_WIDTH),
        ],
        scratch_shapes=[pltpu.VMEM((BF16_SUBLANES, LANES), F32)],
        compiler_params=pltpu.CompilerParams(
            dimension_semantics=("arbitrary", "arbitrary"), vmem_limit_bytes=VMEM_LIMIT),
        name="inproj",
    )(x, gain, shift, scale, w_f, w_gt, w_d, b_f, fqg, fkg, dqg, dkg)


def _causal(s):
    row = lax.broadcasted_iota(jnp.int32, s.shape, 0)
    col = lax.broadcasted_iota(jnp.int32, s.shape, 1)
    return jnp.where(col <= row, s, NEG)


def _split_halves(q):
    lane = lax.broadcasted_iota(jnp.int32, q.shape, 1)
    lo = lane < HEAD_DIM
    zero = jnp.zeros_like(q)
    return lo, jnp.where(lo, q, zero), jnp.where(lo, zero, q)


def _flash(qs, k_ref, v_ref, bias_fn, values_fn, qi, t):
    row0 = qi * t
    m2 = [jnp.full((t, 1), NEG, F32)] * 2
    acc2 = [None, None]
    end = row0 + t
    for key0 in range(0, end, KEY_TILE):
        width = min(KEY_TILE, end - key0)
        keys = slice(key0, key0 + width)
        k = k_ref[keys, :]
        v2 = values_fn(v_ref[keys, :])
        for n, (q, bias) in enumerate(zip(qs, bias_fn(key0, width))):
            s = lax.dot_general(q, k, _NT, preferred_element_type=F32) + bias
            if key0 + width > row0:
                row = lax.broadcasted_iota(jnp.int32, s.shape, 0)
                col = lax.broadcasted_iota(jnp.int32, s.shape, 1)
                s = jnp.where(col <= row + (row0 - key0), s, NEG)
            m_new = jnp.maximum(m2[n], jnp.max(s, axis=-1, keepdims=True))
            pv = _dot(jnp.exp2(s - m_new).astype(BF16), v2[n])
            acc2[n] = pv if acc2[n] is None else acc2[n] * jnp.exp2(m2[n] - m_new) + pv
            m2[n] = m_new
    return acc2


def _fox_kernel(q_ref, k_ref, v_ref, kb_ref, o_ref):
    t = ATTN_TILE

    def bias_fn(start, width):
        return kb_ref[0:1, start:start + width], kb_ref[1:2, start:start + width]

    def values_fn(v):
        lo = lax.broadcasted_iota(jnp.int32, v.shape, 1) < HEAD_DIM
        one = jnp.ones_like(v)
        return jnp.where(lo, v, one), jnp.where(lo, one, v)

    for qi in range(q_ref.shape[0] // t):
        rows = slice(qi * t, (qi + 1) * t)
        lo, q0, q1 = _split_halves(q_ref[rows, :])
        acc0, acc1 = _flash((q0, q1), k_ref, v_ref, bias_fn, values_fn, qi, t)
        out = jnp.where(lo, acc0, acc1)
        sums = pltpu.roll(jnp.where(lo, acc1, acc0), HEAD_DIM, axis=1)
        o_ref[rows, :] = (out / sums).astype(o_ref.dtype)


def _diff_kernel(q_ref, k_ref, v_ref, lam_ref, ng_ref, o_ref, *, lam_init):
    head = pl.program_id(1)
    t = ATTN_TILE
    slope = jnp.exp2(-2.0 * (jnp.full((1, 1), head, jnp.int32) + 1).astype(F32))
    lv = lam_ref[...]
    lam = (jnp.exp(jnp.sum(lv[0:1] * lv[1:2], axis=-1, keepdims=True))
           - jnp.exp(jnp.sum(lv[2:3] * lv[3:4], axis=-1, keepdims=True)) + lam_init)

    def bias_fn(start, width):
        pos = (start + lax.broadcasted_iota(jnp.int32, (1, width), 1)).astype(F32)
        kb = (LOG2E * slope) * pos
        return kb, kb

    def values_fn(v):
        v_ones = jnp.concatenate([v, jnp.ones_like(v)], axis=1)
        return v_ones, v_ones

    for qi in range(q_ref.shape[0] // t):
        rows = slice(qi * t, (qi + 1) * t)
        _, q1, q2 = _split_halves(q_ref[rows, :])
        acc1, acc2 = _flash((q1, q2), k_ref, v_ref, bias_fn, values_fn, qi, t)
        o = acc1[:, :LANES] / acc1[:, LANES:] - lam * (acc2[:, :LANES] / acc2[:, LANES:])
        ms = jnp.mean(o * o, axis=-1, keepdims=True)
        o_ref[rows, :] = ((o * lax.rsqrt(ms + EPS) * ng_ref[...]) * (1.0 - lam_init)).astype(o_ref.dtype)


def _attn_kernel(fq_ref, fk_ref, fv_ref, kb_ref, dq_ref, dk_ref, dv_ref, lam_ref, ng_ref,
                 fox_ref, diff_ref, *, lam_init):
    _fox_kernel(fq_ref, fk_ref, fv_ref, kb_ref, fox_ref)
    _diff_kernel(dq_ref, dk_ref, dv_ref, lam_ref, ng_ref, diff_ref, lam_init=lam_init)


def _attention(fq, fk, fv, kb, dq, dk, dv, lam_params, norm_gain, lam_init):
    batch, seq, width = fq.shape
    groups = width // LANES
    assert dq.shape[2] // LANES == groups
    kb = kb.reshape(batch, groups, 2, seq)
    seq_block = pl.BlockSpec((None, seq, LANES), lambda b, h: (b, 0, h))
    out = jax.ShapeDtypeStruct((batch, seq, width), BF16)
    return pl.pallas_call(
        functools.partial(_attn_kernel, lam_init=lam_init),
        grid=(batch, groups),
        in_specs=[seq_block, seq_block, seq_block,
                  pl.BlockSpec((None, None, 2, seq), lambda b, h: (b, h, 0, 0)),
                  seq_block, seq_block, seq_block,
                  pl.BlockSpec(lam_params.shape, lambda b, h: (0, 0)),
                  pl.BlockSpec(norm_gain.shape, lambda b, h: (0, 0))],
        out_specs=[seq_block, seq_block],
        out_shape=[out, out],
        compiler_params=pltpu.CompilerParams(dimension_semantics=("arbitrary", "arbitrary")),
        name="attention",
    )(fq, fk, fv, kb, dq, dk, dv, lam_params, norm_gain)


def _mlp_kernel(fox_ref, diff_ref, x_ref, wo_ref, g1_ref, ln_ref, sh_ref, sc_ref, g2_ref,
                wg_ref, wu_ref, wd_ref, o_ref):
    mixed = _dot(fox_ref[...], wo_ref[0:FOX_WIDTH, :]) + _dot(diff_ref[...], wo_ref[FOX_WIDTH:, :])
    x1 = x_ref[...] + g1_ref[...] * mixed
    h = _rms_modulate(x1, ln_ref[...], sh_ref[...], sc_ref[...]).astype(BF16)
    d_ff = wg_ref.shape[1]
    tiles = d_ff // MXU_WIDTH
    bounds = [(tiles * c // FFN_CHUNKS) * MXU_WIDTH for c in range(FFN_CHUNKS)] + [d_ff]
    y = None
    for c in range(FFN_CHUNKS):
        cols = slice(bounds[c], bounds[c + 1])
        gate = _dot(h, wg_ref[:, cols])
        up = _dot(h, wu_ref[:, cols])
        a = ((gate / (1.0 + jnp.exp(-gate))) * up).astype(BF16)
        part = _dot(a, wd_ref[cols, :])
        y = part if y is None else y + part
    o_ref[...] = x1 + g2_ref[...] * y


def _mlp(fox, diff, x, w_o, g1, ln_g, shift, scale, g2, w_gate, w_up, w_down):
    batch, seq, d = x.shape
    tm = TOKEN_TILE
    tok = lambda width: pl.BlockSpec((None, tm, width), lambda b, s: (b, s, 0))
    vec = lambda width: pl.BlockSpec((None, 1, width), lambda b, s: (b, 0, 0))
    const = lambda shape: pl.BlockSpec(shape, lambda b, s: (0,) * len(shape),
                                       pipeline_mode=pl.Buffered(1))
    return pl.pallas_call(
        _mlp_kernel,
        grid=(batch, seq // tm),
        in_specs=[
            tok(FOX_WIDTH), tok(DIFF_WIDTH), tok(d), const(w_o.shape), vec(d), const((1, d)),
            vec(d), vec(d), vec(d), const(w_gate.shape), const(w_up.shape), const(w_down.shape),
        ],
        out_specs=tok(d),
        out_shape=jax.ShapeDtypeStruct((batch, seq, d), F32),
        compiler_params=pltpu.CompilerParams(
            dimension_semantics=("arbitrary", "arbitrary"), vmem_limit_bytes=VMEM_LIMIT),
        name="out_mlp",
    )(fox, diff, x, w_o, g1, ln_g, shift, scale, g2, w_gate, w_up, w_down)


def kernel(x, c, ln1_g, ln2_g, w_ada, b_ada, w_in, b_f, fox_qk_g, diff_qk_g, diff_lam, diff_norm_g,
           w_out, w_gate, w_up, w_down):
    depth = w_in.shape[0]
    batch, _, d = x.shape
    mod = _ada(c, w_ada, b_ada)
    fox_cols = 3 * FOX_WIDTH
    w_in = w_in.astype(BF16)
    for l in range(depth):
        sh1, sc1, g1, sh2, sc2, g2 = (
            mod[l, :, i * d:(i + 1) * d].reshape(batch, 1, d) for i in range(MOD_CHUNKS))
        w_f = w_in[l, :, :fox_cols]
        w_d = w_in[l, :, fox_cols + FOX_HEADS:]
        w_gt = jnp.zeros((BF16_SUBLANES, d), BF16).at[:FOX_HEADS].set(
            w_in[l, :, fox_cols:fox_cols + FOX_HEADS].T)
        b_fp = jnp.zeros((BF16_SUBLANES, 1), F32).at[:FOX_HEADS, 0].set(b_f[l])
        per_head = lambda g, n: jnp.tile(g, n).reshape(1, n * HEAD_DIM)
        fq, fk, fv, kb, dq, dk, dv = _inproj(
            x, ln1_g[l].reshape(1, d), sh1, sc1, w_f, w_gt, w_d, b_fp,
            per_head(fox_qk_g[l, 0], FOX_HEADS), per_head(fox_qk_g[l, 1], FOX_HEADS),
            per_head(diff_qk_g[l, 0], 2 * DIFF_HEADS), per_head(diff_qk_g[l, 1], 2 * DIFF_HEADS))
        lam_init = 0.8 - 0.6 * math.exp(-0.3 * l)
        fox, diff = _attention(fq, fk, fv, kb, dq, dk, dv, diff_lam[l],
                               diff_norm_g[l].reshape(1, 2 * HEAD_DIM), lam_init)
        x = _mlp(fox, diff, x, w_out[l].astype(BF16), g1, ln2_g[l].reshape(1, d), sh2, sc2, g2,
                 w_gate[l].astype(BF16), w_up[l].astype(BF16), w_down[l].astype(BF16))
    return x
```

```python
import functools
import math

import jax
import jax.numpy as jnp
from jax import lax
from jax.experimental import pallas as pl
from jax.experimental.pallas import tpu as pltpu

D_MODEL = 1024
HEAD_DIM = 64
FOX_HEADS = 8
DIFF_HEADS = 4
FOX_WIDTH = FOX_HEADS * HEAD_DIM
DIFF_WIDTH = DIFF_HEADS * 2 * HEAD_DIM
MOD_CHUNKS = 6
EPS = 1e-6

LANES = 128
MXU_WIDTH = 256
BF16_SUBLANES = 16
LOG2E = math.log2(math.e)
QK_SCALE = LOG2E / math.sqrt(HEAD_DIM)
NEG = -1e30

TOKEN_TILE = 512
ATTN_TILE = 512
FFN_CHUNKS = 2
VMEM_LIMIT = 56 * 1024 * 1024

BF16 = jnp.bfloat16
F32 = jnp.float32

_NT = (((1,), (1,)), ((), ()))


def _dot(a, b):
    return jnp.dot(a, b, preferred_element_type=F32)


def _rms_modulate(x, gain, shift, scale):
    ms = jnp.mean(x * x, axis=-1, keepdims=True)
    return (x * lax.rsqrt(ms + EPS) * gain) * (1.0 + scale) + shift


def _ada_kernel(c_ref, w_ref, b_ref, o_ref):
    c = c_ref[...]
    cond = c / (1.0 + jnp.exp(-c))
    o_ref[...] = _dot(cond.astype(BF16), w_ref[...].astype(BF16)) + b_ref[...]


def _ada(c, w_ada, b_ada):
    depth, d, n = w_ada.shape
    batch = c.shape[0]
    nb = n // d
    return pl.pallas_call(
        _ada_kernel,
        grid=(depth, nb),
        in_specs=[
            pl.BlockSpec((batch, d), lambda l, j: (0, 0)),
            pl.BlockSpec((None, d, d), lambda l, j: (l, 0, j)),
            pl.BlockSpec((None, 1, d), lambda l, j: (l, 0, j)),
        ],
        out_specs=pl.BlockSpec((None, batch, d), lambda l, j: (l, 0, j)),
        out_shape=jax.ShapeDtypeStruct((depth, batch, n), F32),
        name="ada_mod",
    )(c, w_ada, b_ada.reshape(depth, 1, n))


def _head_rms(u, gain):
    rows, width = u.shape
    lane = lax.broadcasted_iota(jnp.int32, (rows, LANES), 1)
    lo = lane < HEAD_DIM
    outs = []
    for g in range(width // LANES):
        blk = u[:, g * LANES:(g + 1) * LANES]
        sq = blk * blk
        ss_lo = jnp.sum(jnp.where(lo, sq, 0.0), axis=-1, keepdims=True)
        ss_hi = jnp.sum(jnp.where(lo, 0.0, sq), axis=-1, keepdims=True)
        r = jnp.where(lo, lax.rsqrt(ss_lo / HEAD_DIM + EPS), lax.rsqrt(ss_hi / HEAD_DIM + EPS))
        outs.append(blk * r * gain[:, g * LANES:(g + 1) * LANES])
    return jnp.concatenate(outs, axis=-1)


def _inproj_kernel(x_ref, g_ref, sh_ref, sc_ref, wf_ref, wg_ref, wd_ref, bf_ref,
                   fqg_ref, fkg_ref, dqg_ref, dkg_ref,
                   fq_ref, fk_ref, fv_ref, kb_ref, dq_ref, dk_ref, dv_ref, carry_ref):
    si = pl.program_id(1)
    tm = x_ref.shape[0]
    h = _rms_modulate(x_ref[...], g_ref[...], sh_ref[...], sc_ref[...]).astype(BF16)

    w = FOX_WIDTH
    fq_ref[...] = _head_rms(_dot(h, wf_ref[:, 0:w]), fqg_ref[...] * QK_SCALE).astype(BF16)
    fk_ref[...] = _head_rms(_dot(h, wf_ref[:, w:2 * w]), fkg_ref[...]).astype(BF16)
    fv_ref[...] = _dot(h, wf_ref[:, 2 * w:3 * w]).astype(BF16)
    w = DIFF_WIDTH
    dq_ref[...] = _head_rms(_dot(h, wd_ref[:, 0:w]), dqg_ref[...] * QK_SCALE).astype(BF16)
    dk_ref[...] = _head_rms(_dot(h, wd_ref[:, w:2 * w]), dkg_ref[...]).astype(BF16)
    dv_ref[...] = _dot(h, wd_ref[:, 2 * w:3 * w]).astype(BF16)

    z = lax.dot_general(wg_ref[...], h, _NT, preferred_element_type=F32) + bf_ref[...]
    logf = jnp.minimum(z, 0.0) - jnp.log1p(jnp.exp(-jnp.abs(z)))

    hi = logf.astype(BF16)
    r1 = logf - hi.astype(F32)
    mid = r1.astype(BF16)
    low = (r1 - mid.astype(F32)).astype(BF16)
    src = lax.broadcasted_iota(jnp.int32, (tm, tm), 0)
    dst = lax.broadcasted_iota(jnp.int32, (tm, tm), 1)
    tri = jnp.where(src <= dst, 1.0, 0.0).astype(BF16)
    local = _dot(hi, tri) + _dot(mid, tri) + _dot(low, tri)

    @pl.when(si == 0)
    def _():
        carry_ref[...] = jnp.zeros_like(carry_ref)

    carry = carry_ref[:, 0:1]
    cum = carry + local
    kb_ref[...] = (-LOG2E) * cum[0:FOX_HEADS, :]
    carry_ref[...] = jnp.broadcast_to(carry + jnp.sum(logf, axis=-1, keepdims=True), carry_ref.shape)


def _inproj(layer, x, gain, shift, scale, w_f, w_gt, w_d, b_f, fqg, fkg, dqg, dkg):
    batch, seq, d = x.shape
    tm = TOKEN_TILE
    tok = lambda width: pl.BlockSpec((None, tm, width), lambda b, s: (b, s, 0))
    vec = lambda width: pl.BlockSpec((None, 1, width), lambda b, s: (b, 0, 0))
    const = lambda shape: pl.BlockSpec(shape, lambda b, s: (0,) * len(shape),
                                       pipeline_mode=pl.Buffered(1))
    stacked = lambda shape: pl.BlockSpec((None,) + tuple(shape[1:]), lambda b, s: (layer, 0, 0),
                                         pipeline_mode=pl.Buffered(1))
    act = lambda width: jax.ShapeDtypeStruct((batch, seq, width), BF16)
    return pl.pallas_call(
        _inproj_kernel,
        grid=(batch, seq // tm),
        in_specs=[
            tok(d), const((1, d)), vec(d), vec(d),
            stacked(w_f.shape), const(w_gt.shape), stacked(w_d.shape), const(b_f.shape),
            const((1, FOX_WIDTH)), const((1, FOX_WIDTH)), const((1, DIFF_WIDTH)), const((1, DIFF_WIDTH)),
        ],
        out_specs=[
            tok(FOX_WIDTH), tok(FOX_WIDTH), tok(FOX_WIDTH),
            pl.BlockSpec((None, FOX_HEADS, tm), lambda b, s: (b, 0, s)),
            tok(DIFF_WIDTH), tok(DIFF_WIDTH), tok(DIFF_WIDTH),
        ],
        out_shape=[
            act(FOX_WIDTH), act(FOX_WIDTH), act(FOX_WIDTH),
            jax.ShapeDtypeStruct((batch, FOX_HEADS, seq), F32),
            act(DIFF_WIDTH), act(DIFF_WIDTH), act(DIFF_WIDTH),
        ],
        scratch_shapes=[pltpu.VMEM((BF16_SUBLANES, LANES), F32)],
        compiler_params=pltpu.CompilerParams(
            dimension_semantics=("arbitrary", "arbitrary"), vmem_limit_bytes=VMEM_LIMIT),
        name="inproj",
    )(x, gain, shift, scale, w_f, w_gt, w_d, b_f, fqg, fkg, dqg, dkg)


def _causal(s):
    row = lax.broadcasted_iota(jnp.int32, s.shape, 0)
    col = lax.broadcasted_iota(jnp.int32, s.shape, 1)
    return jnp.where(col <= row, s, NEG)


def _split_halves(q):
    lane = lax.broadcasted_iota(jnp.int32, q.shape, 1)
    lo = lane < HEAD_DIM
    zero = jnp.zeros_like(q)
    return lo, jnp.where(lo, q, zero), jnp.where(lo, zero, q)


def _flash(qs, k_ref, v_ref, bias_fn, values_fn, qi, t):
    row0 = qi * t
    half = t // 2
    steps = [(0, key0, t, 0) for key0 in range(0, row0, t)]
    steps += [(0, row0, half, half), (half, row0 + half, half, half)]
    m2 = [jnp.full((t, 1), NEG, F32)] * 2
    acc2 = [None, None]
    for r0, key0, width, tri in steps:
        keys = slice(key0, key0 + width)
        k = k_ref[keys, :]
        v2 = values_fn(v_ref[keys, :])
        for n, (q, bias) in enumerate(zip(qs, bias_fn(key0, width))):
            s = lax.dot_general(q[r0:], k, _NT, preferred_element_type=F32) + bias
            if tri:
                s = jnp.concatenate([_causal(s[:tri]), s[tri:]], axis=0) if tri < s.shape[0] else _causal(s)
            m_old = m2[n][r0:]
            m_new = jnp.maximum(m_old, jnp.max(s, axis=-1, keepdims=True))
            pv = _dot(jnp.exp2(s - m_new).astype(BF16), v2[n])
            acc = pv if acc2[n] is None else acc2[n][r0:] * jnp.exp2(m_old - m_new) + pv
            if r0:
                acc = jnp.concatenate([acc2[n][:r0], acc], axis=0)
                m_new = jnp.concatenate([m2[n][:r0], m_new], axis=0)
            acc2[n], m2[n] = acc, m_new
    return acc2


def _fox_kernel(q_ref, k_ref, v_ref, kb_ref, o_ref):
    t = ATTN_TILE

    def bias_fn(start, width):
        return kb_ref[0:1, start:start + width], kb_ref[1:2, start:start + width]

    def values_fn(v):
        lo = lax.broadcasted_iota(jnp.int32, v.shape, 1) < HEAD_DIM
        one = jnp.ones_like(v)
        return jnp.where(lo, v, one), jnp.where(lo, one, v)

    for qi in range(q_ref.shape[0] // t):
        rows = slice(qi * t, (qi + 1) * t)
        lo, q0, q1 = _split_halves(q_ref[rows, :])
        acc0, acc1 = _flash((q0, q1), k_ref, v_ref, bias_fn, values_fn, qi, t)
        out = jnp.where(lo, acc0, acc1)
        sums = pltpu.roll(jnp.where(lo, acc1, acc0), HEAD_DIM, axis=1)
        o_ref[rows, :] = (out / sums).astype(o_ref.dtype)


def _diff_kernel(q_ref, k_ref, v_ref, lam_ref, ng_ref, o_ref, *, lam_init):
    head = pl.program_id(1)
    t = ATTN_TILE
    slope = jnp.exp2(-2.0 * (jnp.full((1, 1), head, jnp.int32) + 1).astype(F32))
    lv = lam_ref[...]
    lam = (jnp.exp(jnp.sum(lv[0:1] * lv[1:2], axis=-1, keepdims=True))
           - jnp.exp(jnp.sum(lv[2:3] * lv[3:4], axis=-1, keepdims=True)) + lam_init)

    def bias_fn(start, width):
        pos = (start + lax.broadcasted_iota(jnp.int32, (1, width), 1)).astype(F32)
        kb = (LOG2E * slope) * pos
        return kb, kb

    def values_fn(v):
        v_ones = jnp.concatenate([v, jnp.ones_like(v)], axis=1)
        return v_ones, v_ones

    for qi in range(q_ref.shape[0] // t):
        rows = slice(qi * t, (qi + 1) * t)
        _, q1, q2 = _split_halves(q_ref[rows, :])
        acc1, acc2 = _flash((q1, q2), k_ref, v_ref, bias_fn, values_fn, qi, t)
        o = acc1[:, :LANES] / acc1[:, LANES:] - lam * (acc2[:, :LANES] / acc2[:, LANES:])
        ms = jnp.mean(o * o, axis=-1, keepdims=True)
        o_ref[rows, :] = ((o * lax.rsqrt(ms + EPS) * ng_ref[...]) * (1.0 - lam_init)).astype(o_ref.dtype)


def _attn_kernel(fq_ref, fk_ref, fv_ref, kb_ref, dq_ref, dk_ref, dv_ref, lam_ref, ng_ref,
                 fox_ref, diff_ref, *, lam_init):
    _fox_kernel(fq_ref, fk_ref, fv_ref, kb_ref, fox_ref)
    _diff_kernel(dq_ref, dk_ref, dv_ref, lam_ref, ng_ref, diff_ref, lam_init=lam_init)


def _attention(fq, fk, fv, kb, dq, dk, dv, lam_params, norm_gain, lam_init):
    batch, seq, width = fq.shape
    groups = width // LANES
    assert dq.shape[2] // LANES == groups
    kb = kb.reshape(batch, groups, 2, seq)
    seq_block = pl.BlockSpec((None, seq, LANES), lambda b, h: (b, 0, h))
    out = jax.ShapeDtypeStruct((batch, seq, width), BF16)
    return pl.pallas_call(
        functools.partial(_attn_kernel, lam_init=lam_init),
        grid=(batch, groups),
        in_specs=[seq_block, seq_block, seq_block,
                  pl.BlockSpec((None, None, 2, seq), lambda b, h: (b, h, 0, 0)),
                  seq_block, seq_block, seq_block,
                  pl.BlockSpec(lam_params.shape, lambda b, h: (0, 0)),
                  pl.BlockSpec(norm_gain.shape, lambda b, h: (0, 0))],
        out_specs=[seq_block, seq_block],
        out_shape=[out, out],
        compiler_params=pltpu.CompilerParams(dimension_semantics=("arbitrary", "arbitrary")),
        name="attention",
    )(fq, fk, fv, kb, dq, dk, dv, lam_params, norm_gain)


def _mlp_kernel(fox_ref, diff_ref, x_ref, wo_ref, g1_ref, ln_ref, sh_ref, sc_ref, g2_ref,
                wg_ref, wu_ref, wd_ref, o_ref):
    mixed = _dot(fox_ref[...], wo_ref[0:FOX_WIDTH, :]) + _dot(diff_ref[...], wo_ref[FOX_WIDTH:, :])
    x1 = x_ref[...] + g1_ref[...] * mixed
    h = _rms_modulate(x1, ln_ref[...], sh_ref[...], sc_ref[...]).astype(BF16)
    d_ff = wg_ref.shape[1]
    tiles = d_ff // MXU_WIDTH
    bounds = [(tiles * c // FFN_CHUNKS) * MXU_WIDTH for c in range(FFN_CHUNKS)] + [d_ff]
    y = None
    for c in range(FFN_CHUNKS):
        cols = slice(bounds[c], bounds[c + 1])
        gate = _dot(h, wg_ref[:, cols])
        up = _dot(h, wu_ref[:, cols])
        a = ((gate / (1.0 + jnp.exp(-gate))) * up).astype(BF16)
        part = _dot(a, wd_ref[cols, :])
        y = part if y is None else y + part
    o_ref[...] = x1 + g2_ref[...] * y


def _mlp(layer, fox, diff, x, w_o, g1, ln_g, shift, scale, g2, w_gate, w_up, w_down):
    batch, seq, d = x.shape
    tm = TOKEN_TILE
    tok = lambda width: pl.BlockSpec((None, tm, width), lambda b, s: (b, s, 0))
    vec = lambda width: pl.BlockSpec((None, 1, width), lambda b, s: (b, 0, 0))
    const = lambda shape: pl.BlockSpec(shape, lambda b, s: (0,) * len(shape),
                                       pipeline_mode=pl.Buffered(1))
    stacked = lambda shape: pl.BlockSpec((None,) + tuple(shape[1:]), lambda b, s: (layer, 0, 0),
                                         pipeline_mode=pl.Buffered(1))
    return pl.pallas_call(
        _mlp_kernel,
        grid=(batch, seq // tm),
        in_specs=[
            tok(FOX_WIDTH), tok(DIFF_WIDTH), tok(d), stacked(w_o.shape), vec(d), const((1, d)),
            vec(d), vec(d), vec(d), stacked(w_gate.shape), stacked(w_up.shape), stacked(w_down.shape),
        ],
        out_specs=tok(d),
        out_shape=jax.ShapeDtypeStruct((batch, seq, d), F32),
        compiler_params=pltpu.CompilerParams(
            dimension_semantics=("arbitrary", "arbitrary"), vmem_limit_bytes=VMEM_LIMIT),
        name="out_mlp",
    )(fox, diff, x, w_o, g1, ln_g, shift, scale, g2, w_gate, w_up, w_down)


def kernel(x, c, ln1_g, ln2_g, w_ada, b_ada, w_in, b_f, fox_qk_g, diff_qk_g, diff_lam, diff_norm_g,
           w_out, w_gate, w_up, w_down):
    depth = w_in.shape[0]
    batch, _, d = x.shape
    mod = _ada(c, w_ada, b_ada)
    fox_cols = 3 * FOX_WIDTH
    w_f = w_in[:, :, :fox_cols].astype(BF16)
    w_d = w_in[:, :, fox_cols + FOX_HEADS:].astype(BF16)
    w_o, w_g, w_u, w_dn = (w.astype(BF16) for w in (w_out, w_gate, w_up, w_down))
    for l in range(depth):
        sh1, sc1, g1, sh2, sc2, g2 = (
            mod[l, :, i * d:(i + 1) * d].reshape(batch, 1, d) for i in range(MOD_CHUNKS))
        w_gt = jnp.zeros((BF16_SUBLANES, d), BF16).at[:FOX_HEADS].set(
            w_in[l, :, fox_cols:fox_cols + FOX_HEADS].T.astype(BF16))
        b_fp = jnp.zeros((BF16_SUBLANES, 1), F32).at[:FOX_HEADS, 0].set(b_f[l])
        per_head = lambda g, n: jnp.tile(g, n).reshape(1, n * HEAD_DIM)
        fq, fk, fv, kb, dq, dk, dv = _inproj(
            l, x, ln1_g[l].reshape(1, d), sh1, sc1, w_f, w_gt, w_d, b_fp,
            per_head(fox_qk_g[l, 0], FOX_HEADS), per_head(fox_qk_g[l, 1], FOX_HEADS),
            per_head(diff_qk_g[l, 0], 2 * DIFF_HEADS), per_head(diff_qk_g[l, 1], 2 * DIFF_HEADS))
        lam_init = 0.8 - 0.6 * math.exp(-0.3 * l)
        fox, diff = _attention(fq, fk, fv, kb, dq, dk, dv, diff_lam[l],
                               diff_norm_g[l].reshape(1, 2 * HEAD_DIM), lam_init)
        x = _mlp(l, fox, diff, x, w_o, g1, ln2_g[l].reshape(1, d), sh2, sc2, g2, w_g, w_u, w_dn)
    return x
```

```python
import functools
import math

import jax
import jax.numpy as jnp
from jax import lax
from jax.experimental import pallas as pl
from jax.experimental.pallas import tpu as pltpu

D_MODEL = 1024
HEAD_DIM = 64
FOX_HEADS = 8
DIFF_HEADS = 4
FOX_WIDTH = FOX_HEADS * HEAD_DIM
DIFF_WIDTH = DIFF_HEADS * 2 * HEAD_DIM
MOD_CHUNKS = 6
EPS = 1e-6

LANES = 128
MXU_WIDTH = 256
BF16_SUBLANES = 16
LOG2E = math.log2(math.e)
QK_SCALE = LOG2E / math.sqrt(HEAD_DIM)
NEG = -1e30

TOKEN_TILE = 512
INPROJ_TILE = 1024
CUMSUM_CHUNK = 512
ATTN_TILE = 512
FFN_CHUNKS = 2
VMEM_LIMIT = 56 * 1024 * 1024

BF16 = jnp.bfloat16
F32 = jnp.float32

_NT = (((1,), (1,)), ((), ()))


def _dot(a, b):
    return jnp.dot(a, b, preferred_element_type=F32)


def _rms_modulate(x, gain, shift, scale):
    ms = jnp.mean(x * x, axis=-1, keepdims=True)
    return (x * lax.rsqrt(ms + EPS) * gain) * (1.0 + scale) + shift


def _ada_kernel(c_ref, w_ref, b_ref, o_ref):
    c = c_ref[...]
    cond = c / (1.0 + jnp.exp(-c))
    o_ref[...] = _dot(cond.astype(BF16), w_ref[...].astype(BF16)) + b_ref[...]


def _ada(c, w_ada, b_ada):
    depth, d, n = w_ada.shape
    batch = c.shape[0]
    nb = n // d
    return pl.pallas_call(
        _ada_kernel,
        grid=(depth, nb),
        in_specs=[
            pl.BlockSpec((batch, d), lambda l, j: (0, 0)),
            pl.BlockSpec((None, d, d), lambda l, j: (l, 0, j)),
            pl.BlockSpec((None, 1, d), lambda l, j: (l, 0, j)),
        ],
        out_specs=pl.BlockSpec((None, batch, d), lambda l, j: (l, 0, j)),
        out_shape=jax.ShapeDtypeStruct((depth, batch, n), F32),
        name="ada_mod",
    )(c, w_ada, b_ada.reshape(depth, 1, n))


def _head_rms(u, gain):
    rows, width = u.shape
    lane = lax.broadcasted_iota(jnp.int32, (rows, LANES), 1)
    lo = lane < HEAD_DIM
    outs = []
    for g in range(width // LANES):
        blk = u[:, g * LANES:(g + 1) * LANES]
        sq = blk * blk
        ss_lo = jnp.sum(jnp.where(lo, sq, 0.0), axis=-1, keepdims=True)
        ss_hi = jnp.sum(jnp.where(lo, 0.0, sq), axis=-1, keepdims=True)
        r = jnp.where(lo, lax.rsqrt(ss_lo / HEAD_DIM + EPS), lax.rsqrt(ss_hi / HEAD_DIM + EPS))
        outs.append(blk * r * gain[:, g * LANES:(g + 1) * LANES])
    return jnp.concatenate(outs, axis=-1)


def _inproj_kernel(x_ref, g_ref, sh_ref, sc_ref, wf_ref, wg_ref, wd_ref, bf_ref,
                   fqg_ref, fkg_ref, dqg_ref, dkg_ref,
                   fq_ref, fk_ref, fv_ref, kb_ref, dq_ref, dk_ref, dv_ref, carry_ref):
    si = pl.program_id(1)
    tm = x_ref.shape[0]
    h = _rms_modulate(x_ref[...], g_ref[...], sh_ref[...], sc_ref[...]).astype(BF16)

    w = FOX_WIDTH
    fq_ref[...] = _head_rms(_dot(h, wf_ref[:, 0:w]), fqg_ref[...] * QK_SCALE).astype(BF16)
    fk_ref[...] = _head_rms(_dot(h, wf_ref[:, w:2 * w]), fkg_ref[...]).astype(BF16)
    fv_ref[...] = _dot(h, wf_ref[:, 2 * w:3 * w]).astype(BF16)
    w = DIFF_WIDTH
    dq_ref[...] = _head_rms(_dot(h, wd_ref[:, 0:w]), dqg_ref[...] * QK_SCALE).astype(BF16)
    dk_ref[...] = _head_rms(_dot(h, wd_ref[:, w:2 * w]), dkg_ref[...]).astype(BF16)
    dv_ref[...] = _dot(h, wd_ref[:, 2 * w:3 * w]).astype(BF16)

    z = lax.dot_general(wg_ref[...], h, _NT, preferred_element_type=F32) + bf_ref[...]
    logf = jnp.minimum(z, 0.0) - jnp.log1p(jnp.exp(-jnp.abs(z)))

    @pl.when(si == 0)
    def _():
        carry_ref[...] = jnp.zeros_like(carry_ref)

    cw = CUMSUM_CHUNK
    src = lax.broadcasted_iota(jnp.int32, (cw, cw), 0)
    dst = lax.broadcasted_iota(jnp.int32, (cw, cw), 1)
    tri = jnp.where(src <= dst, 1.0, 0.0).astype(BF16)
    carry = carry_ref[:, 0:1]
    for c0 in range(0, tm, cw):
        seg = logf[:, c0:c0 + cw]
        hi = seg.astype(BF16)
        r1 = seg - hi.astype(F32)
        mid = r1.astype(BF16)
        low = (r1 - mid.astype(F32)).astype(BF16)
        cum = carry + (_dot(hi, tri) + _dot(mid, tri) + _dot(low, tri))
        kb_ref[:, c0:c0 + cw] = (-LOG2E) * cum[0:FOX_HEADS, :]
        carry = carry + jnp.sum(seg, axis=-1, keepdims=True)
    carry_ref[...] = jnp.broadcast_to(carry, carry_ref.shape)


def _inproj(layer, x, gain, shift, scale, w_f, w_gt, w_d, b_f, fqg, fkg, dqg, dkg):
    batch, seq, d = x.shape
    tm = INPROJ_TILE
    tok = lambda width: pl.BlockSpec((None, tm, width), lambda b, s: (b, s, 0))
    vec = lambda width: pl.BlockSpec((None, 1, width), lambda b, s: (b, 0, 0))
    const = lambda shape: pl.BlockSpec(shape, lambda b, s: (0,) * len(shape),
                                       pipeline_mode=pl.Buffered(1))
    stacked = lambda shape: pl.BlockSpec((None,) + tuple(shape[1:]), lambda b, s: (layer, 0, 0),
                                         pipeline_mode=pl.Buffered(1))
    act = lambda width: jax.ShapeDtypeStruct((batch, seq, width), BF16)
    return pl.pallas_call(
        _inproj_kernel,
        grid=(batch, seq // tm),
        in_specs=[
            tok(d), const((1, d)), vec(d), vec(d),
            stacked(w_f.shape), const(w_gt.shape), stacked(w_d.shape), const(b_f.shape),
            const((1, FOX_WIDTH)), const((1, FOX_WIDTH)), const((1, DIFF_WIDTH)), const((1, DIFF_WIDTH)),
        ],
        out_specs=[
            tok(FOX_WIDTH), tok(FOX_WIDTH), tok(FOX_WIDTH),
            pl.BlockSpec((None, FOX_HEADS, tm), lambda b, s: (b, 0, s)),
            tok(DIFF_WIDTH), tok(DIFF_WIDTH), tok(DIFF_WIDTH),
        ],
        out_shape=[
            act(FOX_WIDTH), act(FOX_WIDTH), act(FOX_WIDTH),
            jax.ShapeDtypeStruct((batch, FOX_HEADS, seq), F32),
            act(DIFF_WIDTH), act(DIFF_WIDTH), act(DIFF_WIDTH),
        ],
        scratch_shapes=[pltpu.VMEM((BF16_SUBLANES, LANES), F32)],
        compiler_params=pltpu.CompilerParams(
            dimension_semantics=("arbitrary", "arbitrary"), vmem_limit_bytes=VMEM_LIMIT),
        name="inproj",
    )(x, gain, shift, scale, w_f, w_gt, w_d, b_f, fqg, fkg, dqg, dkg)


def _causal(s):
    row = lax.broadcasted_iota(jnp.int32, s.shape, 0)
    col = lax.broadcasted_iota(jnp.int32, s.shape, 1)
    return jnp.where(col <= row, s, NEG)


def _split_halves(q):
    lane = lax.broadcasted_iota(jnp.int32, q.shape, 1)
    lo = lane < HEAD_DIM
    zero = jnp.zeros_like(q)
    return lo, jnp.where(lo, q, zero), jnp.where(lo, zero, q)


def _flash(qs, k_ref, v_ref, bias_fn, qi, t):
    row0 = qi * t
    half = t // 2
    steps = [(0, key0, t, 0) for key0 in range(0, row0, t)]
    steps += [(0, row0, half, half), (half, row0 + half, half, half)]
    m2 = [jnp.full((t, 1), NEG, F32)] * 2
    acc2 = [None, None]
    for r0, key0, width, tri in steps:
        keys = slice(key0, key0 + width)
        k = k_ref[keys, :]
        v = v_ref[keys, :]
        v_ones = jnp.concatenate([v, jnp.ones_like(v)], axis=1)
        rows = t - r0
        s2 = lax.dot_general(jnp.concatenate([q[r0:] for q in qs], axis=0), k, _NT,
                             preferred_element_type=F32)
        ps, alphas = [], []
        for n, bias in enumerate(bias_fn(key0, width)):
            s = s2[n * rows:(n + 1) * rows] + bias
            if tri:
                s = jnp.concatenate([_causal(s[:tri]), s[tri:]], axis=0) if tri < s.shape[0] else _causal(s)
            m_old = m2[n][r0:]
            m_new = jnp.maximum(m_old, jnp.max(s, axis=-1, keepdims=True))
            ps.append(jnp.exp2(s - m_new).astype(BF16))
            alphas.append(jnp.exp2(m_old - m_new))
            m2[n] = m_new if r0 == 0 else jnp.concatenate([m2[n][:r0], m_new], axis=0)
        pv2 = _dot(jnp.concatenate(ps, axis=0), v_ones)
        for n in range(2):
            pv = pv2[n * rows:(n + 1) * rows]
            acc = pv if acc2[n] is None else acc2[n][r0:] * alphas[n] + pv
            acc2[n] = acc if r0 == 0 else jnp.concatenate([acc2[n][:r0], acc], axis=0)
    return acc2


def _fox_kernel(q_ref, k_ref, v_ref, kb_ref, o_ref):
    t = ATTN_TILE

    def bias_fn(start, width):
        return kb_ref[0:1, start:start + width], kb_ref[1:2, start:start + width]

    for qi in range(q_ref.shape[0] // t):
        rows = slice(qi * t, (qi + 1) * t)
        lo, q0, q1 = _split_halves(q_ref[rows, :])
        acc0, acc1 = _flash((q0, q1), k_ref, v_ref, bias_fn, qi, t)
        out = jnp.where(lo, acc0[:, :LANES], acc1[:, :LANES])
        sums = jnp.where(lo, acc0[:, LANES:], acc1[:, LANES:])
        o_ref[rows, :] = (out / sums).astype(o_ref.dtype)


def _diff_kernel(q_ref, k_ref, v_ref, lam_ref, ng_ref, o_ref, *, lam_init):
    head = pl.program_id(1)
    t = ATTN_TILE
    slope = jnp.exp2(-2.0 * (jnp.full((1, 1), head, jnp.int32) + 1).astype(F32))
    lv = lam_ref[...]
    lam = (jnp.exp(jnp.sum(lv[0:1] * lv[1:2], axis=-1, keepdims=True))
           - jnp.exp(jnp.sum(lv[2:3] * lv[3:4], axis=-1, keepdims=True)) + lam_init)

    def bias_fn(start, width):
        pos = (start + lax.broadcasted_iota(jnp.int32, (1, width), 1)).astype(F32)
        kb = (LOG2E * slope) * pos
        return kb, kb

    for qi in range(q_ref.shape[0] // t):
        rows = slice(qi * t, (qi + 1) * t)
        _, q1, q2 = _split_halves(q_ref[rows, :])
        acc1, acc2 = _flash((q1, q2), k_ref, v_ref, bias_fn, qi, t)
        o = acc1[:, :LANES] / acc1[:, LANES:] - lam * (acc2[:, :LANES] / acc2[:, LANES:])
        ms = jnp.mean(o * o, axis=-1, keepdims=True)
        o_ref[rows, :] = ((o * lax.rsqrt(ms + EPS) * ng_ref[...]) * (1.0 - lam_init)).astype(o_ref.dtype)


def _attn_kernel(fq_ref, fk_ref, fv_ref, kb_ref, dq_ref, dk_ref, dv_ref, lam_ref, ng_ref,
                 fox_ref, diff_ref, *, lam_init):
    _fox_kernel(fq_ref, fk_ref, fv_ref, kb_ref, fox_ref)
    _diff_kernel(dq_ref, dk_ref, dv_ref, lam_ref, ng_ref, diff_ref, lam_init=lam_init)


def _attention(fq, fk, fv, kb, dq, dk, dv, lam_params, norm_gain, lam_init):
    batch, seq, width = fq.shape
    groups = width // LANES
    assert dq.shape[2] // LANES == groups
    kb = kb.reshape(batch, groups, 2, seq)
    seq_block = pl.BlockSpec((None, seq, LANES), lambda b, h: (b, 0, h))
    out = jax.ShapeDtypeStruct((batch, seq, width), BF16)
    return pl.pallas_call(
        functools.partial(_attn_kernel, lam_init=lam_init),
        grid=(batch, groups),
        in_specs=[seq_block, seq_block, seq_block,
                  pl.BlockSpec((None, None, 2, seq), lambda b, h: (b, h, 0, 0)),
                  seq_block, seq_block, seq_block,
                  pl.BlockSpec(lam_params.shape, lambda b, h: (0, 0)),
                  pl.BlockSpec(norm_gain.shape, lambda b, h: (0, 0))],
        out_specs=[seq_block, seq_block],
        out_shape=[out, out],
        compiler_params=pltpu.CompilerParams(dimension_semantics=("arbitrary", "arbitrary")),
        name="attention",
    )(fq, fk, fv, kb, dq, dk, dv, lam_params, norm_gain)


def _mlp_kernel(fox_ref, diff_ref, x_ref, wo_ref, g1_ref, ln_ref, sh_ref, sc_ref, g2_ref,
                wg_ref, wu_ref, wd_ref, o_ref):
    mixed = _dot(fox_ref[...], wo_ref[0:FOX_WIDTH, :]) + _dot(diff_ref[...], wo_ref[FOX_WIDTH:, :])
    x1 = x_ref[...] + g1_ref[...] * mixed
    h = _rms_modulate(x1, ln_ref[...], sh_ref[...], sc_ref[...]).astype(BF16)
    d_ff = wg_ref.shape[1]
    tiles = d_ff // MXU_WIDTH
    bounds = [(tiles * c // FFN_CHUNKS) * MXU_WIDTH for c in range(FFN_CHUNKS)] + [d_ff]
    y = None
    for c in range(FFN_CHUNKS):
        cols = slice(bounds[c], bounds[c + 1])
        gate = _dot(h, wg_ref[:, cols])
        up = _dot(h, wu_ref[:, cols])
        a = ((gate / (1.0 + jnp.exp(-gate))) * up).astype(BF16)
        part = _dot(a, wd_ref[cols, :])
        y = part if y is None else y + part
    o_ref[...] = x1 + g2_ref[...] * y


def _mlp(layer, fox, diff, x, w_o, g1, ln_g, shift, scale, g2, w_gate, w_up, w_down):
    batch, seq, d = x.shape
    tm = TOKEN_TILE
    tok = lambda width: pl.BlockSpec((None, tm, width), lambda b, s: (b, s, 0))
    vec = lambda width: pl.BlockSpec((None, 1, width), lambda b, s: (b, 0, 0))
    const = lambda shape: pl.BlockSpec(shape, lambda b, s: (0,) * len(shape),
                                       pipeline_mode=pl.Buffered(1))
    stacked = lambda shape: pl.BlockSpec((None,) + tuple(shape[1:]), lambda b, s: (layer, 0, 0),
                                         pipeline_mode=pl.Buffered(1))
    return pl.pallas_call(
        _mlp_kernel,
        grid=(batch, seq // tm),
        in_specs=[
            tok(FOX_WIDTH), tok(DIFF_WIDTH), tok(d), stacked(w_o.shape), vec(d), const((1, d)),
            vec(d), vec(d), vec(d), stacked(w_gate.shape), stacked(w_up.shape), stacked(w_down.shape),
        ],
        out_specs=tok(d),
        out_shape=jax.ShapeDtypeStruct((batch, seq, d), F32),
        compiler_params=pltpu.CompilerParams(
            dimension_semantics=("arbitrary", "arbitrary"), vmem_limit_bytes=VMEM_LIMIT),
        name="out_mlp",
    )(fox, diff, x, w_o, g1, ln_g, shift, scale, g2, w_gate, w_up, w_down)


def kernel(x, c, ln1_g, ln2_g, w_ada, b_ada, w_in, b_f, fox_qk_g, diff_qk_g, diff_lam, diff_norm_g,
           w_out, w_gate, w_up, w_down):
    depth = w_in.shape[0]
    batch, _, d = x.shape
    mod = _ada(c, w_ada, b_ada)
    fox_cols = 3 * FOX_WIDTH
    w_f = w_in[:, :, :fox_cols].astype(BF16)
    w_d = w_in[:, :, fox_cols + FOX_HEADS:].astype(BF16)
    w_o, w_g, w_u, w_dn = (w.astype(BF16) for w in (w_out, w_gate, w_up, w_down))
    for l in range(depth):
        sh1, sc1, g1, sh2, sc2, g2 = (
            mod[l, :, i * d:(i + 1) * d].reshape(batch, 1, d) for i in range(MOD_CHUNKS))
        w_gt = jnp.zeros((BF16_SUBLANES, d), BF16).at[:FOX_HEADS].set(
            w_in[l, :, fox_cols:fox_cols + FOX_HEADS].T.astype(BF16))
        b_fp = jnp.zeros((BF16_SUBLANES, 1), F32).at[:FOX_HEADS, 0].set(b_f[l])
        per_head = lambda g, n: jnp.tile(g, n).reshape(1, n * HEAD_DIM)
        fq, fk, fv, kb, dq, dk, dv = _inproj(
            l, x, ln1_g[l].reshape(1, d), sh1, sc1, w_f, w_gt, w_d, b_fp,
            per_head(fox_qk_g[l, 0], FOX_HEADS), per_head(fox_qk_g[l, 1], FOX_HEADS),
            per_head(diff_qk_g[l, 0], 2 * DIFF_HEADS), per_head(diff_qk_g[l, 1], 2 * DIFF_HEADS))
        lam_init = 0.8 - 0.6 * math.exp(-0.3 * l)
        fox, diff = _attention(fq, fk, fv, kb, dq, dk, dv, diff_lam[l],
                               diff_norm_g[l].reshape(1, 2 * HEAD_DIM), lam_init)
        x = _mlp(l, fox, diff, x, w_o, g1, ln2_g[l].reshape(1, d), sh2, sc2, g2, w_g, w_u, w_dn)
    return x
```

```python
import functools
import math

import jax
import jax.numpy as jnp
from jax import lax
from jax.experimental import pallas as pl
from jax.experimental.pallas import tpu as pltpu

D_MODEL = 1024
HEAD_DIM = 64
FOX_HEADS = 8
DIFF_HEADS = 4
FOX_WIDTH = FOX_HEADS * HEAD_DIM
DIFF_WIDTH = DIFF_HEADS * 2 * HEAD_DIM
MOD_CHUNKS = 6
EPS = 1e-6

LANES = 128
MXU_WIDTH = 256
BF16_SUBLANES = 16
LOG2E = math.log2(math.e)
QK_SCALE = LOG2E / math.sqrt(HEAD_DIM)
NEG = -1e30

ADA_BLOCK = 2048
TOKEN_TILE = 512
INPROJ_TILE = 1024
CUMSUM_CHUNK = 512
ATTN_TILE = 512
FFN_CHUNKS = 2
VMEM_LIMIT = 56 * 1024 * 1024

BF16 = jnp.bfloat16
F32 = jnp.float32

_NT = (((1,), (1,)), ((), ()))


def _dot(a, b):
    return jnp.dot(a, b, preferred_element_type=F32)


def _rms_modulate(x, gain, shift, scale):
    ms = jnp.mean(x * x, axis=-1, keepdims=True)
    return (x * lax.rsqrt(ms + EPS) * gain) * (1.0 + scale) + shift


def _ada_kernel(c_ref, w_ref, b_ref, o_ref):
    c = c_ref[...]
    cond = c / (1.0 + jnp.exp(-c))
    o_ref[...] = _dot(cond.astype(BF16), w_ref[...].astype(BF16)) + b_ref[...]


def _ada(c, w_ada, b_ada):
    depth, d, n = w_ada.shape
    batch = c.shape[0]
    bn = ADA_BLOCK
    return pl.pallas_call(
        _ada_kernel,
        grid=(depth, n // bn),
        in_specs=[
            pl.BlockSpec((batch, d), lambda l, j: (0, 0)),
            pl.BlockSpec((None, d, bn), lambda l, j: (l, 0, j)),
            pl.BlockSpec((None, 1, bn), lambda l, j: (l, 0, j)),
        ],
        out_specs=pl.BlockSpec((None, batch, bn), lambda l, j: (l, 0, j)),
        out_shape=jax.ShapeDtypeStruct((depth, batch, n), F32),
        name="ada_mod",
    )(c, w_ada, b_ada.reshape(depth, 1, n))


def _split_w_in_kernel(w_ref, wf_ref, wd_ref):
    fox_cols = wf_ref.shape[1]
    wf_ref[...] = w_ref[:, :fox_cols].astype(BF16)
    wd_ref[...] = w_ref[:, fox_cols + FOX_HEADS:].astype(BF16)


def _split_w_in(w_in):
    depth, d, n = w_in.shape
    fox_cols, diff_cols = 3 * FOX_WIDTH, 3 * DIFF_WIDTH
    assert n == fox_cols + FOX_HEADS + diff_cols
    return pl.pallas_call(
        _split_w_in_kernel,
        grid=(depth,),
        in_specs=[pl.BlockSpec((None, d, n), lambda l: (l, 0, 0))],
        out_specs=[pl.BlockSpec((None, d, fox_cols), lambda l: (l, 0, 0)),
                   pl.BlockSpec((None, d, diff_cols), lambda l: (l, 0, 0))],
        out_shape=[jax.ShapeDtypeStruct((depth, d, fox_cols), BF16),
                   jax.ShapeDtypeStruct((depth, d, diff_cols), BF16)],
        compiler_params=pltpu.CompilerParams(vmem_limit_bytes=VMEM_LIMIT),
        name="split_w_in",
    )(w_in)


def _head_rms(u, gain):
    rows, width = u.shape
    lane = lax.broadcasted_iota(jnp.int32, (rows, LANES), 1)
    lo = lane < HEAD_DIM
    outs = []
    for g in range(width // LANES):
        blk = u[:, g * LANES:(g + 1) * LANES]
        sq = blk * blk
        ss_lo = jnp.sum(jnp.where(lo, sq, 0.0), axis=-1, keepdims=True)
        ss_hi = jnp.sum(jnp.where(lo, 0.0, sq), axis=-1, keepdims=True)
        r = jnp.where(lo, lax.rsqrt(ss_lo / HEAD_DIM + EPS), lax.rsqrt(ss_hi / HEAD_DIM + EPS))
        outs.append(blk * r * gain[:, g * LANES:(g + 1) * LANES])
    return jnp.concatenate(outs, axis=-1)


def _inproj_kernel(x_ref, g_ref, sh_ref, sc_ref, wf_ref, wg_ref, wd_ref, bf_ref,
                   fqg_ref, fkg_ref, dqg_ref, dkg_ref,
                   fq_ref, fk_ref, fv_ref, kb_ref, dq_ref, dk_ref, dv_ref, carry_ref):
    si = pl.program_id(1)
    tm = x_ref.shape[0]
    h = _rms_modulate(x_ref[...], g_ref[...], sh_ref[...], sc_ref[...]).astype(BF16)

    w = FOX_WIDTH
    fq_ref[...] = _head_rms(_dot(h, wf_ref[:, 0:w]), fqg_ref[...] * QK_SCALE).astype(BF16)
    fk_ref[...] = _head_rms(_dot(h, wf_ref[:, w:2 * w]), fkg_ref[...]).astype(BF16)
    fv_ref[...] = _dot(h, wf_ref[:, 2 * w:3 * w]).astype(BF16)
    w = DIFF_WIDTH
    dq_ref[...] = _head_rms(_dot(h, wd_ref[:, 0:w]), dqg_ref[...] * QK_SCALE).astype(BF16)
    dk_ref[...] = _head_rms(_dot(h, wd_ref[:, w:2 * w]), dkg_ref[...]).astype(BF16)
    dv_ref[...] = _dot(h, wd_ref[:, 2 * w:3 * w]).astype(BF16)

    z = lax.dot_general(wg_ref[...], h, _NT, preferred_element_type=F32) + bf_ref[...]
    logf = jnp.minimum(z, 0.0) - jnp.log1p(jnp.exp(-jnp.abs(z)))

    @pl.when(si == 0)
    def _():
        carry_ref[...] = jnp.zeros_like(carry_ref)

    cw = CUMSUM_CHUNK
    src = lax.broadcasted_iota(jnp.int32, (cw, cw), 0)
    dst = lax.broadcasted_iota(jnp.int32, (cw, cw), 1)
    tri = jnp.where(src <= dst, 1.0, 0.0).astype(BF16)
    carry = carry_ref[:, 0:1]
    for c0 in range(0, tm, cw):
        seg = logf[:, c0:c0 + cw]
        hi = seg.astype(BF16)
        r1 = seg - hi.astype(F32)
        mid = r1.astype(BF16)
        low = (r1 - mid.astype(F32)).astype(BF16)
        cum = carry + (_dot(hi, tri) + _dot(mid, tri) + _dot(low, tri))
        kb_ref[:, c0:c0 + cw] = (-LOG2E) * cum[0:FOX_HEADS, :]
        carry = carry + jnp.sum(seg, axis=-1, keepdims=True)
    carry_ref[...] = jnp.broadcast_to(carry, carry_ref.shape)


def _inproj(layer, x, gain, shift, scale, w_f, w_gt, w_d, b_f, fqg, fkg, dqg, dkg):
    batch, seq, d = x.shape
    tm = INPROJ_TILE
    tok = lambda width: pl.BlockSpec((None, tm, width), lambda b, s: (b, s, 0))
    vec = lambda width: pl.BlockSpec((None, 1, width), lambda b, s: (b, 0, 0))
    const = lambda shape: pl.BlockSpec(shape, lambda b, s: (0,) * len(shape),
                                       pipeline_mode=pl.Buffered(1))
    stacked = lambda shape: pl.BlockSpec((None,) + tuple(shape[1:]), lambda b, s: (layer, 0, 0),
                                         pipeline_mode=pl.Buffered(1))
    act = lambda width: jax.ShapeDtypeStruct((batch, seq, width), BF16)
    return pl.pallas_call(
        _inproj_kernel,
        grid=(batch, seq // tm),
        in_specs=[
            tok(d), const((1, d)), vec(d), vec(d),
            stacked(w_f.shape), const(w_gt.shape), stacked(w_d.shape), const(b_f.shape),
            const((1, FOX_WIDTH)), const((1, FOX_WIDTH)), const((1, DIFF_WIDTH)), const((1, DIFF_WIDTH)),
        ],
        out_specs=[
            tok(FOX_WIDTH), tok(FOX_WIDTH), tok(FOX_WIDTH),
            pl.BlockSpec((None, FOX_HEADS, tm), lambda b, s: (b, 0, s)),
            tok(DIFF_WIDTH), tok(DIFF_WIDTH), tok(DIFF_WIDTH),
        ],
        out_shape=[
            act(FOX_WIDTH), act(FOX_WIDTH), act(FOX_WIDTH),
            jax.ShapeDtypeStruct((batch, FOX_HEADS, seq), F32),
            act(DIFF_WIDTH), act(DIFF_WIDTH), act(DIFF_WIDTH),
        ],
        scratch_shapes=[pltpu.VMEM((BF16_SUBLANES, LANES), F32)],
        compiler_params=pltpu.CompilerParams(
            dimension_semantics=("arbitrary", "arbitrary"), vmem_limit_bytes=VMEM_LIMIT),
        name="inproj",
    )(x, gain, shift, scale, w_f, w_gt, w_d, b_f, fqg, fkg, dqg, dkg)


def _causal(s):
    row = lax.broadcasted_iota(jnp.int32, s.shape, 0)
    col = lax.broadcasted_iota(jnp.int32, s.shape, 1)
    return jnp.where(col <= row, s, NEG)


def _split_halves(q):
    lane = lax.broadcasted_iota(jnp.int32, q.shape, 1)
    lo = lane < HEAD_DIM
    zero = jnp.zeros_like(q)
    return lo, jnp.where(lo, q, zero), jnp.where(lo, zero, q)


def _flash(qs, k_ref, v_ref, bias_fn, qi, t):
    row0 = qi * t
    half = t // 2
    steps = [(0, key0, t, 0) for key0 in range(0, row0, t)]
    steps += [(0, row0, half, half), (half, row0 + half, half, half)]
    m2 = [jnp.full((t, 1), NEG, F32)] * 2
    acc2 = [None, None]
    for r0, key0, width, tri in steps:
        keys = slice(key0, key0 + width)
        k = k_ref[keys, :]
        v = v_ref[keys, :]
        v_ones = jnp.concatenate([v, jnp.ones_like(v)], axis=1)
        rows = t - r0
        s2 = lax.dot_general(jnp.concatenate([q[r0:] for q in qs], axis=0), k, _NT,
                             preferred_element_type=F32)
        ps, alphas = [], []
        for n, bias in enumerate(bias_fn(key0, width)):
            s = s2[n * rows:(n + 1) * rows] + bias
            if tri:
                s = jnp.concatenate([_causal(s[:tri]), s[tri:]], axis=0) if tri < s.shape[0] else _causal(s)
            m_old = m2[n][r0:]
            m_new = jnp.maximum(m_old, jnp.max(s, axis=-1, keepdims=True))
            ps.append(jnp.exp2(s - m_new).astype(BF16))
            alphas.append(jnp.exp2(m_old - m_new))
            m2[n] = m_new if r0 == 0 else jnp.concatenate([m2[n][:r0], m_new], axis=0)
        pv2 = _dot(jnp.concatenate(ps, axis=0), v_ones)
        for n in range(2):
            pv = pv2[n * rows:(n + 1) * rows]
            acc = pv if acc2[n] is None else acc2[n][r0:] * alphas[n] + pv
            acc2[n] = acc if r0 == 0 else jnp.concatenate([acc2[n][:r0], acc], axis=0)
    return acc2


def _fox_kernel(q_ref, k_ref, v_ref, kb_ref, o_ref):
    t = ATTN_TILE

    def bias_fn(start, width):
        return kb_ref[0:1, start:start + width], kb_ref[1:2, start:start + width]

    for qi in range(q_ref.shape[0] // t):
        rows = slice(qi * t, (qi + 1) * t)
        lo, q0, q1 = _split_halves(q_ref[rows, :])
        acc0, acc1 = _flash((q0, q1), k_ref, v_ref, bias_fn, qi, t)
        out = jnp.where(lo, acc0[:, :LANES], acc1[:, :LANES])
        sums = jnp.where(lo, acc0[:, LANES:], acc1[:, LANES:])
        o_ref[rows, :] = (out / sums).astype(o_ref.dtype)


def _diff_kernel(q_ref, k_ref, v_ref, lam_ref, ng_ref, o_ref, *, lam_init):
    head = pl.program_id(1)
    t = ATTN_TILE
    slope = jnp.exp2(-2.0 * (jnp.full((1, 1), head, jnp.int32) + 1).astype(F32))
    lv = lam_ref[...]
    lam = (jnp.exp(jnp.sum(lv[0:1] * lv[1:2], axis=-1, keepdims=True))
           - jnp.exp(jnp.sum(lv[2:3] * lv[3:4], axis=-1, keepdims=True)) + lam_init)

    def bias_fn(start, width):
        pos = (start + lax.broadcasted_iota(jnp.int32, (1, width), 1)).astype(F32)
        kb = (LOG2E * slope) * pos
        return kb, kb

    for qi in range(q_ref.shape[0] // t):
        rows = slice(qi * t, (qi + 1) * t)
        _, q1, q2 = _split_halves(q_ref[rows, :])
        acc1, acc2 = _flash((q1, q2), k_ref, v_ref, bias_fn, qi, t)
        o = acc1[:, :LANES] / acc1[:, LANES:] - lam * (acc2[:, :LANES] / acc2[:, LANES:])
        ms = jnp.mean(o * o, axis=-1, keepdims=True)
        o_ref[rows, :] = ((o * lax.rsqrt(ms + EPS) * ng_ref[...]) * (1.0 - lam_init)).astype(o_ref.dtype)


def _attn_kernel(fq_ref, fk_ref, fv_ref, kb_ref, dq_ref, dk_ref, dv_ref, lam_ref, ng_ref,
                 fox_ref, diff_ref, *, lam_init):
    _fox_kernel(fq_ref, fk_ref, fv_ref, kb_ref, fox_ref)
    _diff_kernel(dq_ref, dk_ref, dv_ref, lam_ref, ng_ref, diff_ref, lam_init=lam_init)


def _attention(fq, fk, fv, kb, dq, dk, dv, lam_params, norm_gain, lam_init):
    batch, seq, width = fq.shape
    groups = width // LANES
    assert dq.shape[2] // LANES == groups
    kb = kb.reshape(batch, groups, 2, seq)
    seq_block = pl.BlockSpec((None, seq, LANES), lambda b, h: (b, 0, h))
    out = jax.ShapeDtypeStruct((batch, seq, width), BF16)
    return pl.pallas_call(
        functools.partial(_attn_kernel, lam_init=lam_init),
        grid=(batch, groups),
        in_specs=[seq_block, seq_block, seq_block,
                  pl.BlockSpec((None, None, 2, seq), lambda b, h: (b, h, 0, 0)),
                  seq_block, seq_block, seq_block,
                  pl.BlockSpec(lam_params.shape, lambda b, h: (0, 0)),
                  pl.BlockSpec(norm_gain.shape, lambda b, h: (0, 0))],
        out_specs=[seq_block, seq_block],
        out_shape=[out, out],
        compiler_params=pltpu.CompilerParams(dimension_semantics=("arbitrary", "arbitrary")),
        name="attention",
    )(fq, fk, fv, kb, dq, dk, dv, lam_params, norm_gain)


def _mlp_kernel(fox_ref, diff_ref, x_ref, wo_ref, g1_ref, ln_ref, sh_ref, sc_ref, g2_ref,
                wg_ref, wu_ref, wd_ref, o_ref):
    mixed = _dot(fox_ref[...], wo_ref[0:FOX_WIDTH, :]) + _dot(diff_ref[...], wo_ref[FOX_WIDTH:, :])
    x1 = x_ref[...] + g1_ref[...] * mixed
    h = _rms_modulate(x1, ln_ref[...], sh_ref[...], sc_ref[...]).astype(BF16)
    d_ff = wg_ref.shape[1]
    tiles = d_ff // MXU_WIDTH
    bounds = [(tiles * c // FFN_CHUNKS) * MXU_WIDTH for c in range(FFN_CHUNKS)] + [d_ff]
    y = None
    for c in range(FFN_CHUNKS):
        cols = slice(bounds[c], bounds[c + 1])
        gate = _dot(h, wg_ref[:, cols])
        up = _dot(h, wu_ref[:, cols])
        a = ((gate / (1.0 + jnp.exp(-gate))) * up).astype(BF16)
        part = _dot(a, wd_ref[cols, :])
        y = part if y is None else y + part
    o_ref[...] = x1 + g2_ref[...] * y


def _mlp(layer, fox, diff, x, w_o, g1, ln_g, shift, scale, g2, w_gate, w_up, w_down):
    batch, seq, d = x.shape
    tm = TOKEN_TILE
    tok = lambda width: pl.BlockSpec((None, tm, width), lambda b, s: (b, s, 0))
    vec = lambda width: pl.BlockSpec((None, 1, width), lambda b, s: (b, 0, 0))
    const = lambda shape: pl.BlockSpec(shape, lambda b, s: (0,) * len(shape),
                                       pipeline_mode=pl.Buffered(1))
    stacked = lambda shape: pl.BlockSpec((None,) + tuple(shape[1:]), lambda b, s: (layer, 0, 0),
                                         pipeline_mode=pl.Buffered(1))
    return pl.pallas_call(
        _mlp_kernel,
        grid=(batch, seq // tm),
        in_specs=[
            tok(FOX_WIDTH), tok(DIFF_WIDTH), tok(d), stacked(w_o.shape), vec(d), const((1, d)),
            vec(d), vec(d), vec(d), stacked(w_gate.shape), stacked(w_up.shape), stacked(w_down.shape),
        ],
        out_specs=tok(d),
        out_shape=jax.ShapeDtypeStruct((batch, seq, d), F32),
        compiler_params=pltpu.CompilerParams(
            dimension_semantics=("arbitrary", "arbitrary"), vmem_limit_bytes=VMEM_LIMIT),
        name="out_mlp",
    )(fox, diff, x, w_o, g1, ln_g, shift, scale, g2, w_gate, w_up, w_down)


def kernel(x, c, ln1_g, ln2_g, w_ada, b_ada, w_in, b_f, fox_qk_g, diff_qk_g, diff_lam, diff_norm_g,
           w_out, w_gate, w_up, w_down):
    depth = w_in.shape[0]
    batch, _, d = x.shape
    mod = _ada(c, w_ada, b_ada)
    fox_cols = 3 * FOX_WIDTH
    w_f, w_d = _split_w_in(w_in)
    w_o, w_g, w_u, w_dn = (w.astype(BF16) for w in (w_out, w_gate, w_up, w_down))
    for l in range(depth):
        sh1, sc1, g1, sh2, sc2, g2 = (
            mod[l, :, i * d:(i + 1) * d].reshape(batch, 1, d) for i in range(MOD_CHUNKS))
        w_gt = jnp.zeros((BF16_SUBLANES, d), BF16).at[:FOX_HEADS].set(
            w_in[l, :, fox_cols:fox_cols + FOX_HEADS].T.astype(BF16))
        b_fp = jnp.zeros((BF16_SUBLANES, 1), F32).at[:FOX_HEADS, 0].set(b_f[l])
        per_head = lambda g, n: jnp.tile(g, n).reshape(1, n * HEAD_DIM)
        fq, fk, fv, kb, dq, dk, dv = _inproj(
            l, x, ln1_g[l].reshape(1, d), sh1, sc1, w_f, w_gt, w_d, b_fp,
            per_head(fox_qk_g[l, 0], FOX_HEADS), per_head(fox_qk_g[l, 1], FOX_HEADS),
            per_head(diff_qk_g[l, 0], 2 * DIFF_HEADS), per_head(diff_qk_g[l, 1], 2 * DIFF_HEADS))
        lam_init = 0.8 - 0.6 * math.exp(-0.3 * l)
        fox, diff = _attention(fq, fk, fv, kb, dq, dk, dv, diff_lam[l],
                               diff_norm_g[l].reshape(1, 2 * HEAD_DIM), lam_init)
        x = _mlp(l, fox, diff, x, w_o, g1, ln2_g[l].reshape(1, d), sh2, sc2, g2, w_g, w_u, w_dn)
    return x
```

```python
import functools
import math

import jax
import jax.numpy as jnp
from jax import lax
from jax.experimental import pallas as pl
from jax.experimental.pallas import tpu as pltpu

D_MODEL = 1024
HEAD_DIM = 64
FOX_HEADS = 8
DIFF_HEADS = 4
FOX_WIDTH = FOX_HEADS * HEAD_DIM
DIFF_WIDTH = DIFF_HEADS * 2 * HEAD_DIM
MOD_CHUNKS = 6
EPS = 1e-6

LANES = 128
MXU_WIDTH = 256
BF16_SUBLANES = 16
LOG2E = math.log2(math.e)
QK_SCALE = LOG2E / math.sqrt(HEAD_DIM)
NEG = -1e30

TOKEN_TILE = 1024
CUMSUM_CHUNK = 512
ATTN_TILE = 512
FFN_CHUNKS = 4
VMEM_LIMIT = 56 * 1024 * 1024

BF16 = jnp.bfloat16
F32 = jnp.float32

_NT = (((1,), (1,)), ((), ()))


def _dot(a, b):
    return jnp.dot(a, b, preferred_element_type=F32)


def _rms_modulate(x, gain, shift, scale):
    ms = jnp.mean(x * x, axis=-1, keepdims=True)
    return (x * lax.rsqrt(ms + EPS) * gain) * (1.0 + scale) + shift


def _ada_kernel(c_ref, w_ref, b_ref, o_ref):
    c = c_ref[...]
    cond = c / (1.0 + jnp.exp(-c))
    o_ref[...] = _dot(cond.astype(BF16), w_ref[...].astype(BF16)) + b_ref[...]


def _ada(c, w_ada, b_ada):
    depth, d, n = w_ada.shape
    batch = c.shape[0]
    nb = n // d
    return pl.pallas_call(
        _ada_kernel,
        grid=(depth, nb),
        in_specs=[
            pl.BlockSpec((batch, d), lambda l, j: (0, 0)),
            pl.BlockSpec((None, d, d), lambda l, j: (l, 0, j)),
            pl.BlockSpec((None, 1, d), lambda l, j: (l, 0, j)),
        ],
        out_specs=pl.BlockSpec((None, batch, d), lambda l, j: (l, 0, j)),
        out_shape=jax.ShapeDtypeStruct((depth, batch, n), F32),
        name="ada_mod",
    )(c, w_ada, b_ada.reshape(depth, 1, n))


def _head_rms(u, gain):
    rows, width = u.shape
    lane = lax.broadcasted_iota(jnp.int32, (rows, LANES), 1)
    lo = lane < HEAD_DIM
    outs = []
    for g in range(width // LANES):
        blk = u[:, g * LANES:(g + 1) * LANES]
        sq = blk * blk
        ss_lo = jnp.sum(jnp.where(lo, sq, 0.0), axis=-1, keepdims=True)
        ss_hi = jnp.sum(jnp.where(lo, 0.0, sq), axis=-1, keepdims=True)
        r = jnp.where(lo, lax.rsqrt(ss_lo / HEAD_DIM + EPS), lax.rsqrt(ss_hi / HEAD_DIM + EPS))
        outs.append(blk * r * gain[:, g * LANES:(g + 1) * LANES])
    return jnp.concatenate(outs, axis=-1)


def _inproj_kernel(x_ref, g_ref, sh_ref, sc_ref, wf_ref, wg_ref, wd_ref, bf_ref,
                   fqg_ref, fkg_ref, dqg_ref, dkg_ref,
                   fq_ref, fk_ref, fv_ref, kb_ref, dq_ref, dk_ref, dv_ref, carry_ref):
    si = pl.program_id(1)
    tm = x_ref.shape[0]
    h = _rms_modulate(x_ref[...], g_ref[...], sh_ref[...], sc_ref[...]).astype(BF16)

    w = FOX_WIDTH
    fq_ref[...] = _head_rms(_dot(h, wf_ref[:, 0:w]), fqg_ref[...] * QK_SCALE).astype(BF16)
    fk_ref[...] = _head_rms(_dot(h, wf_ref[:, w:2 * w]), fkg_ref[...]).astype(BF16)
    fv_ref[...] = _dot(h, wf_ref[:, 2 * w:3 * w]).astype(BF16)
    w = DIFF_WIDTH
    dq_ref[...] = _head_rms(_dot(h, wd_ref[:, 0:w]), dqg_ref[...] * QK_SCALE).astype(BF16)
    dk_ref[...] = _head_rms(_dot(h, wd_ref[:, w:2 * w]), dkg_ref[...]).astype(BF16)
    dv_ref[...] = _dot(h, wd_ref[:, 2 * w:3 * w]).astype(BF16)

    z = lax.dot_general(wg_ref[...], h, _NT, preferred_element_type=F32) + bf_ref[...]
    logf = jnp.minimum(z, 0.0) - jnp.log1p(jnp.exp(-jnp.abs(z)))

    @pl.when(si == 0)
    def _():
        carry_ref[...] = jnp.zeros_like(carry_ref)

    cw = CUMSUM_CHUNK
    src = lax.broadcasted_iota(jnp.int32, (cw, cw), 0)
    dst = lax.broadcasted_iota(jnp.int32, (cw, cw), 1)
    tri = jnp.where(src <= dst, 1.0, 0.0).astype(BF16)
    carry = carry_ref[:, 0:1]
    for c0 in range(0, tm, cw):
        seg = logf[:, c0:c0 + cw]
        hi = seg.astype(BF16)
        r1 = seg - hi.astype(F32)
        mid = r1.astype(BF16)
        low = (r1 - mid.astype(F32)).astype(BF16)
        cum = carry + (_dot(hi, tri) + _dot(mid, tri) + _dot(low, tri))
        kb_ref[:, c0:c0 + cw] = (-LOG2E) * cum[0:FOX_HEADS, :]
        carry = carry + jnp.sum(seg, axis=-1, keepdims=True)
    carry_ref[...] = jnp.broadcast_to(carry, carry_ref.shape)


def _inproj(layer, x, gain, shift, scale, w_f, w_gt, w_d, b_f, fqg, fkg, dqg, dkg):
    batch, seq, d = x.shape
    tm = TOKEN_TILE
    tok = lambda width: pl.BlockSpec((None, tm, width), lambda b, s: (b, s, 0))
    vec = lambda width: pl.BlockSpec((None, 1, width), lambda b, s: (b, 0, 0))
    const = lambda shape: pl.BlockSpec(shape, lambda b, s: (0,) * len(shape),
                                       pipeline_mode=pl.Buffered(1))
    stacked = lambda shape: pl.BlockSpec((None,) + tuple(shape[1:]), lambda b, s: (layer, 0, 0),
                                         pipeline_mode=pl.Buffered(1))
    act = lambda width: jax.ShapeDtypeStruct((batch, seq, width), BF16)
    return pl.pallas_call(
        _inproj_kernel,
        grid=(batch, seq // tm),
        in_specs=[
            tok(d), const((1, d)), vec(d), vec(d),
            stacked(w_f.shape), const(w_gt.shape), stacked(w_d.shape), const(b_f.shape),
            const((1, FOX_WIDTH)), const((1, FOX_WIDTH)), const((1, DIFF_WIDTH)), const((1, DIFF_WIDTH)),
        ],
        out_specs=[
            tok(FOX_WIDTH), tok(FOX_WIDTH), tok(FOX_WIDTH),
            pl.BlockSpec((None, FOX_HEADS, tm), lambda b, s: (b, 0, s)),
            tok(DIFF_WIDTH), tok(DIFF_WIDTH), tok(DIFF_WIDTH),
        ],
        out_shape=[
            act(FOX_WIDTH), act(FOX_WIDTH), act(FOX_WIDTH),
            jax.ShapeDtypeStruct((batch, FOX_HEADS, seq), F32),
            act(DIFF_WIDTH), act(DIFF_WIDTH), act(DIFF_WIDTH),
        ],
        scratch_shapes=[pltpu.VMEM((BF16_SUBLANES, LANES), F32)],
        compiler_params=pltpu.CompilerParams(
            dimension_semantics=("arbitrary", "arbitrary"), vmem_limit_bytes=VMEM_LIMIT),
        name="inproj",
    )(x, gain, shift, scale, w_f, w_gt, w_d, b_f, fqg, fkg, dqg, dkg)


def _causal(s):
    row = lax.broadcasted_iota(jnp.int32, s.shape, 0)
    col = lax.broadcasted_iota(jnp.int32, s.shape, 1)
    return jnp.where(col <= row, s, NEG)


def _split_halves(q):
    lane = lax.broadcasted_iota(jnp.int32, q.shape, 1)
    lo = lane < HEAD_DIM
    zero = jnp.zeros_like(q)
    return lo, jnp.where(lo, q, zero), jnp.where(lo, zero, q)


def _flash(qs, k_ref, v_ref, bias_fn, qi, t):
    row0 = qi * t
    half = t // 2
    steps = [(0, key0, t, 0) for key0 in range(0, row0, t)]
    steps += [(0, row0, half, half), (half, row0 + half, half, half)]
    m2 = [jnp.full((t, 1), NEG, F32)] * 2
    acc2 = [None, None]
    for r0, key0, width, tri in steps:
        keys = slice(key0, key0 + width)
        k = k_ref[keys, :]
        v = v_ref[keys, :]
        v_ones = jnp.concatenate([v, jnp.ones_like(v)], axis=1)
        rows = t - r0
        s2 = lax.dot_general(jnp.concatenate([q[r0:] for q in qs], axis=0), k, _NT,
                             preferred_element_type=F32)
        ps, alphas = [], []
        for n, bias in enumerate(bias_fn(key0, width)):
            s = s2[n * rows:(n + 1) * rows] + bias
            if tri:
                s = jnp.concatenate([_causal(s[:tri]), s[tri:]], axis=0) if tri < s.shape[0] else _causal(s)
            m_old = m2[n][r0:]
            m_new = jnp.maximum(m_old, jnp.max(s, axis=-1, keepdims=True))
            ps.append(jnp.exp2(s - m_new).astype(BF16))
            alphas.append(jnp.exp2(m_old - m_new))
            m2[n] = m_new if r0 == 0 else jnp.concatenate([m2[n][:r0], m_new], axis=0)
        pv2 = _dot(jnp.concatenate(ps, axis=0), v_ones)
        for n in range(2):
            pv = pv2[n * rows:(n + 1) * rows]
            acc = pv if acc2[n] is None else acc2[n][r0:] * alphas[n] + pv
            acc2[n] = acc if r0 == 0 else jnp.concatenate([acc2[n][:r0], acc], axis=0)
    return acc2


def _fox_kernel(q_ref, k_ref, v_ref, kb_ref, o_ref):
    t = ATTN_TILE

    def bias_fn(start, width):
        return kb_ref[0:1, start:start + width], kb_ref[1:2, start:start + width]

    for qi in range(q_ref.shape[0] // t):
        rows = slice(qi * t, (qi + 1) * t)
        lo, q0, q1 = _split_halves(q_ref[rows, :])
        acc0, acc1 = _flash((q0, q1), k_ref, v_ref, bias_fn, qi, t)
        out = jnp.where(lo, acc0[:, :LANES], acc1[:, :LANES])
        sums = jnp.where(lo, acc0[:, LANES:], acc1[:, LANES:])
        o_ref[rows, :] = (out / sums).astype(o_ref.dtype)


def _diff_kernel(q_ref, k_ref, v_ref, lam_ref, ng_ref, o_ref, *, lam_init):
    head = pl.program_id(1)
    t = ATTN_TILE
    slope = jnp.exp2(-2.0 * (jnp.full((1, 1), head, jnp.int32) + 1).astype(F32))
    lv = lam_ref[...]
    lam = (jnp.exp(jnp.sum(lv[0:1] * lv[1:2], axis=-1, keepdims=True))
           - jnp.exp(jnp.sum(lv[2:3] * lv[3:4], axis=-1, keepdims=True)) + lam_init)

    def bias_fn(start, width):
        pos = (start + lax.broadcasted_iota(jnp.int32, (1, width), 1)).astype(F32)
        kb = (LOG2E * slope) * pos
        return kb, kb

    for qi in range(q_ref.shape[0] // t):
        rows = slice(qi * t, (qi + 1) * t)
        _, q1, q2 = _split_halves(q_ref[rows, :])
        acc1, acc2 = _flash((q1, q2), k_ref, v_ref, bias_fn, qi, t)
        o = acc1[:, :LANES] / acc1[:, LANES:] - lam * (acc2[:, :LANES] / acc2[:, LANES:])
        ms = jnp.mean(o * o, axis=-1, keepdims=True)
        o_ref[rows, :] = ((o * lax.rsqrt(ms + EPS) * ng_ref[...]) * (1.0 - lam_init)).astype(o_ref.dtype)


def _attn_kernel(fq_ref, fk_ref, fv_ref, kb_ref, dq_ref, dk_ref, dv_ref, lam_ref, ng_ref,
                 fox_ref, diff_ref, *, lam_init):
    _fox_kernel(fq_ref, fk_ref, fv_ref, kb_ref, fox_ref)
    _diff_kernel(dq_ref, dk_ref, dv_ref, lam_ref, ng_ref, diff_ref, lam_init=lam_init)


def _attention(fq, fk, fv, kb, dq, dk, dv, lam_params, norm_gain, lam_init):
    batch, seq, width = fq.shape
    groups = width // LANES
    assert dq.shape[2] // LANES == groups
    kb = kb.reshape(batch, groups, 2, seq)
    seq_block = pl.BlockSpec((None, seq, LANES), lambda b, h: (b, 0, h))
    out = jax.ShapeDtypeStruct((batch, seq, width), BF16)
    return pl.pallas_call(
        functools.partial(_attn_kernel, lam_init=lam_init),
        grid=(batch, groups),
        in_specs=[seq_block, seq_block, seq_block,
                  pl.BlockSpec((None, None, 2, seq), lambda b, h: (b, h, 0, 0)),
                  seq_block, seq_block, seq_block,
                  pl.BlockSpec(lam_params.shape, lambda b, h: (0, 0)),
                  pl.BlockSpec(norm_gain.shape, lambda b, h: (0, 0))],
        out_specs=[seq_block, seq_block],
        out_shape=[out, out],
        compiler_params=pltpu.CompilerParams(dimension_semantics=("arbitrary", "arbitrary")),
        name="attention",
    )(fq, fk, fv, kb, dq, dk, dv, lam_params, norm_gain)


def _mlp_kernel(fox_ref, diff_ref, x_ref, wo_ref, g1_ref, ln_ref, sh_ref, sc_ref, g2_ref,
                wg_ref, wu_ref, wd_ref, o_ref):
    mixed = _dot(fox_ref[...], wo_ref[0:FOX_WIDTH, :]) + _dot(diff_ref[...], wo_ref[FOX_WIDTH:, :])
    x1 = x_ref[...] + g1_ref[...] * mixed
    h = _rms_modulate(x1, ln_ref[...], sh_ref[...], sc_ref[...]).astype(BF16)
    d_ff = wg_ref.shape[1]
    tiles = d_ff // MXU_WIDTH
    bounds = [(tiles * c // FFN_CHUNKS) * MXU_WIDTH for c in range(FFN_CHUNKS)] + [d_ff]
    o_ref[...] = x1
    for c in range(FFN_CHUNKS):
        cols = slice(bounds[c], bounds[c + 1])
        gate = _dot(h, wg_ref[:, cols])
        up = _dot(h, wu_ref[:, cols])
        a = ((gate / (1.0 + jnp.exp(-gate))) * up).astype(BF16)
        o_ref[...] += g2_ref[...] * _dot(a, wd_ref[cols, :])


def _mlp(layer, fox, diff, x, w_o, g1, ln_g, shift, scale, g2, w_gate, w_up, w_down):
    batch, seq, d = x.shape
    tm = TOKEN_TILE
    tok = lambda width: pl.BlockSpec((None, tm, width), lambda b, s: (b, s, 0))
    vec = lambda width: pl.BlockSpec((None, 1, width), lambda b, s: (b, 0, 0))
    const = lambda shape: pl.BlockSpec(shape, lambda b, s: (0,) * len(shape),
                                       pipeline_mode=pl.Buffered(1))
    stacked = lambda shape: pl.BlockSpec((None,) + tuple(shape[1:]), lambda b, s: (layer, 0, 0),
                                         pipeline_mode=pl.Buffered(1))
    return pl.pallas_call(
        _mlp_kernel,
        grid=(batch, seq // tm),
        in_specs=[
            tok(FOX_WIDTH), tok(DIFF_WIDTH), tok(d), stacked(w_o.shape), vec(d), const((1, d)),
            vec(d), vec(d), vec(d), stacked(w_gate.shape), stacked(w_up.shape), stacked(w_down.shape),
        ],
        out_specs=tok(d),
        out_shape=jax.ShapeDtypeStruct((batch, seq, d), F32),
        compiler_params=pltpu.CompilerParams(
            dimension_semantics=("arbitrary", "arbitrary"), vmem_limit_bytes=VMEM_LIMIT),
        name="out_mlp",
    )(fox, diff, x, w_o, g1, ln_g, shift, scale, g2, w_gate, w_up, w_down)


def kernel(x, c, ln1_g, ln2_g, w_ada, b_ada, w_in, b_f, fox_qk_g, diff_qk_g, diff_lam, diff_norm_g,
           w_out, w_gate, w_up, w_down):
    depth = w_in.shape[0]
    batch, _, d = x.shape
    mod = _ada(c, w_ada, b_ada)
    fox_cols = 3 * FOX_WIDTH
    w_f = w_in[:, :, :fox_cols].astype(BF16)
    w_d = w_in[:, :, fox_cols + FOX_HEADS:].astype(BF16)
    w_o, w_g, w_u, w_dn = (w.astype(BF16) for w in (w_out, w_gate, w_up, w_down))
    for l in range(depth):
        sh1, sc1, g1, sh2, sc2, g2 = (
            mod[l, :, i * d:(i + 1) * d].reshape(batch, 1, d) for i in range(MOD_CHUNKS))
        w_gt = jnp.zeros((BF16_SUBLANES, d), BF16).at[:FOX_HEADS].set(
            w_in[l, :, fox_cols:fox_cols + FOX_HEADS].T.astype(BF16))
        b_fp = jnp.zeros((BF16_SUBLANES, 1), F32).at[:FOX_HEADS, 0].set(b_f[l])
        per_head = lambda g, n: jnp.tile(g, n).reshape(1, n * HEAD_DIM)
        fq, fk, fv, kb, dq, dk, dv = _inproj(
            l, x, ln1_g[l].reshape(1, d), sh1, sc1, w_f, w_gt, w_d, b_fp,
            per_head(fox_qk_g[l, 0], FOX_HEADS), per_head(fox_qk_g[l, 1], FOX_HEADS),
            per_head(diff_qk_g[l, 0], 2 * DIFF_HEADS), per_head(diff_qk_g[l, 1], 2 * DIFF_HEADS))
        lam_init = 0.8 - 0.6 * math.exp(-0.3 * l)
        fox, diff = _attention(fq, fk, fv, kb, dq, dk, dv, diff_lam[l],
                               diff_norm_g[l].reshape(1, 2 * HEAD_DIM), lam_init)
        x = _mlp(l, fox, diff, x, w_o, g1, ln2_g[l].reshape(1, d), sh2, sc2, g2, w_g, w_u, w_dn)
    return x
```

```python
import functools
import math

import jax
import jax.numpy as jnp
from jax import lax
from jax.experimental import pallas as pl
from jax.experimental.pallas import tpu as pltpu

D_MODEL = 1024
HEAD_DIM = 64
FOX_HEADS = 8
DIFF_HEADS = 4
FOX_WIDTH = FOX_HEADS * HEAD_DIM
DIFF_WIDTH = DIFF_HEADS * 2 * HEAD_DIM
MOD_CHUNKS = 6
EPS = 1e-6

LANES = 128
MXU_WIDTH = 256
BF16_SUBLANES = 16
LOG2E = math.log2(math.e)
QK_SCALE = LOG2E / math.sqrt(HEAD_DIM)
NEG = -1e30

TOKEN_TILE = 1024
CUMSUM_CHUNK = MXU_WIDTH
ATTN_TILE = 512
FFN_CHUNKS = 4
VMEM_LIMIT = 56 * 1024 * 1024

BF16 = jnp.bfloat16
F32 = jnp.float32

_NT = (((1,), (1,)), ((), ()))


def _dot(a, b):
    return jnp.dot(a, b, preferred_element_type=F32)


def _rms_modulate(x, gain, shift, scale):
    ms = jnp.mean(x * x, axis=-1, keepdims=True)
    return (x * lax.rsqrt(ms + EPS) * gain) * (1.0 + scale) + shift


def _ada_kernel(c_ref, w_ref, b_ref, o_ref):
    c = c_ref[...]
    cond = c / (1.0 + jnp.exp(-c))
    o_ref[...] = _dot(cond.astype(BF16), w_ref[...].astype(BF16)) + b_ref[...]


def _ada(c, w_ada, b_ada):
    depth, d, n = w_ada.shape
    batch = c.shape[0]
    nb = n // d
    return pl.pallas_call(
        _ada_kernel,
        grid=(depth, nb),
        in_specs=[
            pl.BlockSpec((batch, d), lambda l, j: (0, 0)),
            pl.BlockSpec((None, d, d), lambda l, j: (l, 0, j)),
            pl.BlockSpec((None, 1, d), lambda l, j: (l, 0, j)),
        ],
        out_specs=pl.BlockSpec((None, batch, d), lambda l, j: (l, 0, j)),
        out_shape=jax.ShapeDtypeStruct((depth, batch, n), F32),
        name="ada_mod",
    )(c, w_ada, b_ada.reshape(depth, 1, n))


def _head_rms(u, gain):
    rows, width = u.shape
    lane = lax.broadcasted_iota(jnp.int32, (rows, LANES), 1)
    lo = lane < HEAD_DIM
    outs = []
    for g in range(width // LANES):
        blk = u[:, g * LANES:(g + 1) * LANES]
        sq = blk * blk
        ss_lo = jnp.sum(jnp.where(lo, sq, 0.0), axis=-1, keepdims=True)
        ss_hi = jnp.sum(jnp.where(lo, 0.0, sq), axis=-1, keepdims=True)
        r = jnp.where(lo, lax.rsqrt(ss_lo / HEAD_DIM + EPS), lax.rsqrt(ss_hi / HEAD_DIM + EPS))
        outs.append(blk * r * gain[:, g * LANES:(g + 1) * LANES])
    return jnp.concatenate(outs, axis=-1)


def _inproj_kernel(x_ref, g_ref, sh_ref, sc_ref, wf_ref, wg_ref, wd_ref, bf_ref,
                   fqg_ref, fkg_ref, dqg_ref, dkg_ref,
                   fq_ref, fk_ref, fv_ref, kb_ref, dq_ref, dk_ref, dv_ref, carry_ref):
    tm = x_ref.shape[0]

    @pl.when(pl.program_id(1) == 0)
    def _():
        carry_ref[...] = jnp.zeros_like(carry_ref)

    h = _rms_modulate(x_ref[...], g_ref[...], sh_ref[...], sc_ref[...]).astype(BF16)

    w = FOX_WIDTH
    fq_ref[...] = _head_rms(_dot(h, wf_ref[:, 0:w]), fqg_ref[...] * QK_SCALE).astype(BF16)
    fk_ref[...] = _head_rms(_dot(h, wf_ref[:, w:2 * w]), fkg_ref[...]).astype(BF16)
    fv_ref[...] = _dot(h, wf_ref[:, 2 * w:3 * w]).astype(BF16)
    w = DIFF_WIDTH
    dq_ref[...] = _head_rms(_dot(h, wd_ref[:, 0:w]), dqg_ref[...] * QK_SCALE).astype(BF16)
    dk_ref[...] = _head_rms(_dot(h, wd_ref[:, w:2 * w]), dkg_ref[...]).astype(BF16)
    dv_ref[...] = _dot(h, wd_ref[:, 2 * w:3 * w]).astype(BF16)

    z = lax.dot_general(wg_ref[...], h, _NT, preferred_element_type=F32) + bf_ref[...]
    logf = jnp.minimum(z, 0.0) - jnp.log1p(jnp.exp(-jnp.abs(z)))

    cw = CUMSUM_CHUNK
    src = lax.broadcasted_iota(jnp.int32, (cw, cw), 0)
    dst = lax.broadcasted_iota(jnp.int32, (cw, cw), 1)
    tri = jnp.where(src <= dst, 1.0, 0.0).astype(BF16)
    segs = jnp.concatenate([logf[:, c0:c0 + cw] for c0 in range(0, tm, cw)], axis=0)
    hi = segs.astype(BF16)
    r1 = segs - hi.astype(F32)
    mid = r1.astype(BF16)
    low = (r1 - mid.astype(F32)).astype(BF16)
    parts = _dot(jnp.concatenate([hi, mid, low], axis=0), tri)
    n = segs.shape[0]
    local = parts[0:n] + parts[n:2 * n] + parts[2 * n:3 * n]
    totals = jnp.sum(segs, axis=-1, keepdims=True)
    carry = carry_ref[:, 0:1]
    rows = logf.shape[0]
    for c in range(tm // cw):
        blk = slice(c * rows, (c + 1) * rows)
        cum = carry + local[blk]
        kb_ref[:, c * cw:(c + 1) * cw] = (-LOG2E) * cum[0:FOX_HEADS, :]
        carry = carry + totals[blk]
    carry_ref[...] = jnp.broadcast_to(carry, carry_ref.shape)


def _inproj(layer, x, gain, shift, scale, w_f, w_gt, w_d, b_f, fqg, fkg, dqg, dkg):
    batch, seq, d = x.shape
    tm = TOKEN_TILE
    tok = lambda width: pl.BlockSpec((None, tm, width), lambda b, s: (b, s, 0))
    vec = lambda width: pl.BlockSpec((None, 1, width), lambda b, s: (b, 0, 0))
    const = lambda shape: pl.BlockSpec(shape, lambda b, s: (0,) * len(shape),
                                       pipeline_mode=pl.Buffered(1))
    stacked = lambda shape: pl.BlockSpec((None,) + tuple(shape[1:]), lambda b, s: (layer, 0, 0),
                                         pipeline_mode=pl.Buffered(1))
    act = lambda width: jax.ShapeDtypeStruct((batch, seq, width), BF16)
    return pl.pallas_call(
        _inproj_kernel,
        grid=(batch, seq // tm),
        in_specs=[
            tok(d), const((1, d)), vec(d), vec(d),
            stacked(w_f.shape), const(w_gt.shape), stacked(w_d.shape), const(b_f.shape),
            const((1, FOX_WIDTH)), const((1, FOX_WIDTH)), const((1, DIFF_WIDTH)), const((1, DIFF_WIDTH)),
        ],
        out_specs=[
            tok(FOX_WIDTH), tok(FOX_WIDTH), tok(FOX_WIDTH),
            pl.BlockSpec((None, FOX_HEADS, tm), lambda b, s: (b, 0, s)),
            tok(DIFF_WIDTH), tok(DIFF_WIDTH), tok(DIFF_WIDTH),
        ],
        out_shape=[
            act(FOX_WIDTH), act(FOX_WIDTH), act(FOX_WIDTH),
            jax.ShapeDtypeStruct((batch, FOX_HEADS, seq), F32),
            act(DIFF_WIDTH), act(DIFF_WIDTH), act(DIFF_WIDTH),
        ],
        scratch_shapes=[pltpu.VMEM((BF16_SUBLANES, LANES), F32)],
        compiler_params=pltpu.CompilerParams(
            dimension_semantics=("arbitrary", "arbitrary"), vmem_limit_bytes=VMEM_LIMIT),
        name="inproj",
    )(x, gain, shift, scale, w_f, w_gt, w_d, b_f, fqg, fkg, dqg, dkg)


def _causal(s):
    row = lax.broadcasted_iota(jnp.int32, s.shape, 0)
    col = lax.broadcasted_iota(jnp.int32, s.shape, 1)
    return jnp.where(col <= row, s, NEG)


def _split_halves(q):
    lane = lax.broadcasted_iota(jnp.int32, q.shape, 1)
    lo = lane < HEAD_DIM
    zero = jnp.zeros_like(q)
    return lo, jnp.where(lo, q, zero), jnp.where(lo, zero, q)


def _flash(qs, k_ref, v_ref, bias_fn, qi, t):
    row0 = qi * t
    half = t // 2
    steps = [(0, key0, t, 0) for key0 in range(0, row0, t)]
    steps += [(0, row0, half, half), (half, row0 + half, half, half)]
    m2 = [jnp.full((t, 1), NEG, F32)] * 2
    acc2 = [None, None]
    for r0, key0, width, tri in steps:
        keys = slice(key0, key0 + width)
        k = k_ref[keys, :]
        v = v_ref[keys, :]
        v_ones = jnp.concatenate([v, jnp.ones_like(v)], axis=1)
        rows = t - r0
        s2 = lax.dot_general(jnp.concatenate([q[r0:] for q in qs], axis=0), k, _NT,
                             preferred_element_type=F32)
        ps, alphas = [], []
        for n, bias in enumerate(bias_fn(key0, width)):
            s = s2[n * rows:(n + 1) * rows] + bias
            if tri:
                s = jnp.concatenate([_causal(s[:tri]), s[tri:]], axis=0) if tri < s.shape[0] else _causal(s)
            m_old = m2[n][r0:]
            m_new = jnp.maximum(m_old, jnp.max(s, axis=-1, keepdims=True))
            ps.append(jnp.exp2(s - m_new).astype(BF16))
            alphas.append(jnp.exp2(m_old - m_new))
            m2[n] = m_new if r0 == 0 else jnp.concatenate([m2[n][:r0], m_new], axis=0)
        pv2 = _dot(jnp.concatenate(ps, axis=0), v_ones)
        for n in range(2):
            pv = pv2[n * rows:(n + 1) * rows]
            acc = pv if acc2[n] is None else acc2[n][r0:] * alphas[n] + pv
            acc2[n] = acc if r0 == 0 else jnp.concatenate([acc2[n][:r0], acc], axis=0)
    return acc2


def _fox_kernel(q_ref, k_ref, v_ref, kb_ref, o_ref):
    t = ATTN_TILE

    def bias_fn(start, width):
        return kb_ref[0:1, start:start + width], kb_ref[1:2, start:start + width]

    for qi in range(q_ref.shape[0] // t):
        rows = slice(qi * t, (qi + 1) * t)
        lo, q0, q1 = _split_halves(q_ref[rows, :])
        acc0, acc1 = _flash((q0, q1), k_ref, v_ref, bias_fn, qi, t)
        out = jnp.where(lo, acc0[:, :LANES], acc1[:, :LANES])
        sums = jnp.where(lo, acc0[:, LANES:], acc1[:, LANES:])
        o_ref[rows, :] = (out / sums).astype(o_ref.dtype)


def _diff_kernel(q_ref, k_ref, v_ref, lam_ref, ng_ref, o_ref, *, lam_init):
    head = pl.program_id(1)
    t = ATTN_TILE
    slope = jnp.exp2(-2.0 * (jnp.full((1, 1), head, jnp.int32) + 1).astype(F32))
    lv = lam_ref[...]
    lam = (jnp.exp(jnp.sum(lv[0:1] * lv[1:2], axis=-1, keepdims=True))
           - jnp.exp(jnp.sum(lv[2:3] * lv[3:4], axis=-1, keepdims=True)) + lam_init)

    def bias_fn(start, width):
        pos = (start + lax.broadcasted_iota(jnp.int32, (1, width), 1)).astype(F32)
        kb = (LOG2E * slope) * pos
        return kb, kb

    for qi in range(q_ref.shape[0] // t):
        rows = slice(qi * t, (qi + 1) * t)
        _, q1, q2 = _split_halves(q_ref[rows, :])
        acc1, acc2 = _flash((q1, q2), k_ref, v_ref, bias_fn, qi, t)
        o = acc1[:, :LANES] / acc1[:, LANES:] - lam * (acc2[:, :LANES] / acc2[:, LANES:])
        ms = jnp.mean(o * o, axis=-1, keepdims=True)
        o_ref[rows, :] = ((o * lax.rsqrt(ms + EPS) * ng_ref[...]) * (1.0 - lam_init)).astype(o_ref.dtype)


def _attn_kernel(fq_ref, fk_ref, fv_ref, kb_ref, dq_ref, dk_ref, dv_ref, lam_ref, ng_ref,
                 fox_ref, diff_ref, *, lam_init):
    _fox_kernel(fq_ref, fk_ref, fv_ref, kb_ref, fox_ref)
    _diff_kernel(dq_ref, dk_ref, dv_ref, lam_ref, ng_ref, diff_ref, lam_init=lam_init)


def _attention(fq, fk, fv, kb, dq, dk, dv, lam_params, norm_gain, lam_init):
    batch, seq, width = fq.shape
    groups = width // LANES
    assert dq.shape[2] // LANES == groups
    kb = kb.reshape(batch, groups, 2, seq)
    seq_block = pl.BlockSpec((None, seq, LANES), lambda b, h: (b, 0, h))
    out = jax.ShapeDtypeStruct((batch, seq, width), BF16)
    return pl.pallas_call(
        functools.partial(_attn_kernel, lam_init=lam_init),
        grid=(batch, groups),
        in_specs=[seq_block, seq_block, seq_block,
                  pl.BlockSpec((None, None, 2, seq), lambda b, h: (b, h, 0, 0)),
                  seq_block, seq_block, seq_block,
                  pl.BlockSpec(lam_params.shape, lambda b, h: (0, 0)),
                  pl.BlockSpec(norm_gain.shape, lambda b, h: (0, 0))],
        out_specs=[seq_block, seq_block],
        out_shape=[out, out],
        compiler_params=pltpu.CompilerParams(dimension_semantics=("arbitrary", "arbitrary")),
        name="attention",
    )(fq, fk, fv, kb, dq, dk, dv, lam_params, norm_gain)


def _mlp_kernel(fox_ref, diff_ref, x_ref, wo_ref, g1_ref, ln_ref, sh_ref, sc_ref, g2_ref,
                wg_ref, wu_ref, wd_ref, o_ref):
    mixed = _dot(fox_ref[...], wo_ref[0:FOX_WIDTH, :]) + _dot(diff_ref[...], wo_ref[FOX_WIDTH:, :])
    x1 = x_ref[...] + g1_ref[...] * mixed
    h = _rms_modulate(x1, ln_ref[...], sh_ref[...], sc_ref[...]).astype(BF16)
    d_ff = wg_ref.shape[1]
    tiles = d_ff // MXU_WIDTH
    bounds = [(tiles * c // FFN_CHUNKS) * MXU_WIDTH for c in range(FFN_CHUNKS)] + [d_ff]
    o_ref[...] = x1
    for c in range(FFN_CHUNKS):
        cols = slice(bounds[c], bounds[c + 1])
        gate = _dot(h, wg_ref[:, cols])
        up = _dot(h, wu_ref[:, cols])
        a = ((gate / (1.0 + jnp.exp(-gate))) * up).astype(BF16)
        o_ref[...] += g2_ref[...] * _dot(a, wd_ref[cols, :])


def _mlp(layer, fox, diff, x, w_o, g1, ln_g, shift, scale, g2, w_gate, w_up, w_down):
    batch, seq, d = x.shape
    tm = TOKEN_TILE
    tok = lambda width: pl.BlockSpec((None, tm, width), lambda b, s: (b, s, 0))
    vec = lambda width: pl.BlockSpec((None, 1, width), lambda b, s: (b, 0, 0))
    const = lambda shape: pl.BlockSpec(shape, lambda b, s: (0,) * len(shape),
                                       pipeline_mode=pl.Buffered(1))
    stacked = lambda shape: pl.BlockSpec((None,) + tuple(shape[1:]), lambda b, s: (layer, 0, 0),
                                         pipeline_mode=pl.Buffered(1))
    return pl.pallas_call(
        _mlp_kernel,
        grid=(batch, seq // tm),
        in_specs=[
            tok(FOX_WIDTH), tok(DIFF_WIDTH), tok(d), stacked(w_o.shape), vec(d), const((1, d)),
            vec(d), vec(d), vec(d), stacked(w_gate.shape), stacked(w_up.shape), stacked(w_down.shape),
        ],
        out_specs=tok(d),
        out_shape=jax.ShapeDtypeStruct((batch, seq, d), F32),
        compiler_params=pltpu.CompilerParams(
            dimension_semantics=("arbitrary", "arbitrary"), vmem_limit_bytes=VMEM_LIMIT),
        name="out_mlp",
    )(fox, diff, x, w_o, g1, ln_g, shift, scale, g2, w_gate, w_up, w_down)


def kernel(x, c, ln1_g, ln2_g, w_ada, b_ada, w_in, b_f, fox_qk_g, diff_qk_g, diff_lam, diff_norm_g,
           w_out, w_gate, w_up, w_down):
    depth = w_in.shape[0]
    batch, _, d = x.shape
    mod = _ada(c, w_ada, b_ada)
    fox_cols = 3 * FOX_WIDTH
    w_f = w_in[:, :, :fox_cols].astype(BF16)
    w_d = w_in[:, :, fox_cols + FOX_HEADS:].astype(BF16)
    w_o, w_g, w_u, w_dn = (w.astype(BF16) for w in (w_out, w_gate, w_up, w_down))
    for l in range(depth):
        sh1, sc1, g1, sh2, sc2, g2 = (
            mod[l, :, i * d:(i + 1) * d].reshape(batch, 1, d) for i in range(MOD_CHUNKS))
        w_gt = jnp.zeros((BF16_SUBLANES, d), BF16).at[:FOX_HEADS].set(
            w_in[l, :, fox_cols:fox_cols + FOX_HEADS].T.astype(BF16))
        b_fp = jnp.zeros((BF16_SUBLANES, 1), F32).at[:FOX_HEADS, 0].set(b_f[l])
        per_head = lambda g, n: jnp.tile(g, n).reshape(1, n * HEAD_DIM)
        fq, fk, fv, kb, dq, dk, dv = _inproj(
            l, x, ln1_g[l].reshape(1, d), sh1, sc1, w_f, w_gt, w_d, b_fp,
            per_head(fox_qk_g[l, 0], FOX_HEADS), per_head(fox_qk_g[l, 1], FOX_HEADS),
            per_head(diff_qk_g[l, 0], 2 * DIFF_HEADS), per_head(diff_qk_g[l, 1], 2 * DIFF_HEADS))
        lam_init = 0.8 - 0.6 * math.exp(-0.3 * l)
        fox, diff = _attention(fq, fk, fv, kb, dq, dk, dv, diff_lam[l],
                               diff_norm_g[l].reshape(1, 2 * HEAD_DIM), lam_init)
        x = _mlp(l, fox, diff, x, w_o, g1, ln2_g[l].reshape(1, d), sh2, sc2, g2, w_g, w_u, w_dn)
    return x
```

```python
import functools
import math

import jax
import jax.numpy as jnp
from jax import lax
from jax.experimental import pallas as pl
from jax.experimental.pallas import tpu as pltpu

D_MODEL = 1024
HEAD_DIM = 64
FOX_HEADS = 8
DIFF_HEADS = 4
FOX_WIDTH = FOX_HEADS * HEAD_DIM
DIFF_WIDTH = DIFF_HEADS * 2 * HEAD_DIM
MOD_CHUNKS = 6
EPS = 1e-6

LANES = 128
MXU_WIDTH = 256
BF16_SUBLANES = 16
LOG2E = math.log2(math.e)
QK_SCALE = LOG2E / math.sqrt(HEAD_DIM)
NEG = -1e30

TOKEN_TILE = 1024
CUMSUM_CHUNK = MXU_WIDTH
ATTN_TILE = 512
FFN_CHUNKS = 4
VMEM_LIMIT = 56 * 1024 * 1024

BF16 = jnp.bfloat16
F32 = jnp.float32

_NT = (((1,), (1,)), ((), ()))


def _dot(a, b):
    return jnp.dot(a, b, preferred_element_type=F32)


def _rms_modulate(x, gain, shift, scale):
    ms = jnp.mean(x * x, axis=-1, keepdims=True)
    return (x * lax.rsqrt(ms + EPS) * gain) * (1.0 + scale) + shift


def _ada_kernel(c_ref, w_ref, b_ref, o_ref):
    c = c_ref[...]
    cond = c / (1.0 + jnp.exp(-c))
    o_ref[...] = _dot(cond.astype(BF16), w_ref[...].astype(BF16)) + b_ref[...]


def _ada(c, w_ada, b_ada):
    depth, d, n = w_ada.shape
    batch = c.shape[0]
    nb = n // d
    return pl.pallas_call(
        _ada_kernel,
        grid=(depth, nb),
        in_specs=[
            pl.BlockSpec((batch, d), lambda l, j: (0, 0)),
            pl.BlockSpec((None, d, d), lambda l, j: (l, 0, j)),
            pl.BlockSpec((None, 1, d), lambda l, j: (l, 0, j)),
        ],
        out_specs=pl.BlockSpec((None, batch, d), lambda l, j: (l, 0, j)),
        out_shape=jax.ShapeDtypeStruct((depth, batch, n), F32),
        name="ada_mod",
    )(c, w_ada, b_ada.reshape(depth, 1, n))


def _head_rms(u, gain):
    rows, width = u.shape
    lane = lax.broadcasted_iota(jnp.int32, (rows, LANES), 1)
    lo = lane < HEAD_DIM
    outs = []
    for g in range(width // LANES):
        blk = u[:, g * LANES:(g + 1) * LANES]
        sq = blk * blk
        ss_lo = jnp.sum(jnp.where(lo, sq, 0.0), axis=-1, keepdims=True)
        ss_hi = jnp.sum(jnp.where(lo, 0.0, sq), axis=-1, keepdims=True)
        r = jnp.where(lo, lax.rsqrt(ss_lo / HEAD_DIM + EPS), lax.rsqrt(ss_hi / HEAD_DIM + EPS))
        outs.append(blk * r * gain[:, g * LANES:(g + 1) * LANES])
    return jnp.concatenate(outs, axis=-1)


def _inproj_kernel(x_ref, g_ref, sh_ref, sc_ref, wf_ref, wg_ref, wd_ref, bf_ref,
                   fqg_ref, fkg_ref, dqg_ref, dkg_ref,
                   fq_ref, fk_ref, fv_ref, kb_ref, dq_ref, dk_ref, dv_ref, carry_ref):
    tm = x_ref.shape[0]

    @pl.when(pl.program_id(1) == 0)
    def _():
        carry_ref[...] = jnp.zeros_like(carry_ref)

    h = _rms_modulate(x_ref[...], g_ref[...], sh_ref[...], sc_ref[...]).astype(BF16)

    w = FOX_WIDTH
    fq_ref[...] = _head_rms(_dot(h, wf_ref[:, 0:w]), fqg_ref[...] * QK_SCALE).astype(BF16)
    fk_ref[...] = _head_rms(_dot(h, wf_ref[:, w:2 * w]), fkg_ref[...]).astype(BF16)
    fv_ref[...] = _dot(h, wf_ref[:, 2 * w:3 * w]).astype(BF16)
    w = DIFF_WIDTH
    dq_ref[...] = _head_rms(_dot(h, wd_ref[:, 0:w]), dqg_ref[...] * QK_SCALE).astype(BF16)
    dk_ref[...] = _head_rms(_dot(h, wd_ref[:, w:2 * w]), dkg_ref[...]).astype(BF16)
    dv_ref[...] = _dot(h, wd_ref[:, 2 * w:3 * w]).astype(BF16)

    z = lax.dot_general(wg_ref[...], h, _NT, preferred_element_type=F32) + bf_ref[...]
    logf = jnp.minimum(z, 0.0) - jnp.log1p(jnp.exp(-jnp.abs(z)))

    cw = CUMSUM_CHUNK
    src = lax.broadcasted_iota(jnp.int32, (cw, cw), 0)
    dst = lax.broadcasted_iota(jnp.int32, (cw, cw), 1)
    tri = jnp.where(src <= dst, 1.0, 0.0).astype(BF16)
    segs = jnp.concatenate([logf[:, c0:c0 + cw] for c0 in range(0, tm, cw)], axis=0)
    hi = segs.astype(BF16)
    r1 = segs - hi.astype(F32)
    mid = r1.astype(BF16)
    low = (r1 - mid.astype(F32)).astype(BF16)
    parts = _dot(jnp.concatenate([hi, mid, low], axis=0), tri)
    n = segs.shape[0]
    local = parts[0:n] + parts[n:2 * n] + parts[2 * n:3 * n]
    totals = jnp.sum(segs, axis=-1, keepdims=True)
    carry = carry_ref[:, 0:1]
    rows = logf.shape[0]
    for c in range(tm // cw):
        blk = slice(c * rows, (c + 1) * rows)
        cum = carry + local[blk]
        kb_ref[:, c * cw:(c + 1) * cw] = (-LOG2E) * cum[0:FOX_HEADS, :]
        carry = carry + totals[blk]
    carry_ref[...] = jnp.broadcast_to(carry, carry_ref.shape)


def _inproj(layer, x, gain, mod, w_f, w_gt, w_d, b_f, fqg, fkg, dqg, dkg):
    batch, seq, d = x.shape
    tm = TOKEN_TILE
    tok = lambda width: pl.BlockSpec((None, tm, width), lambda b, s: (b, s, 0))
    mod_chunk = lambda i: pl.BlockSpec((None, None, None, 1, d), lambda b, s: (layer, b, i, 0, 0))
    const = lambda shape: pl.BlockSpec(shape, lambda b, s: (0,) * len(shape),
                                       pipeline_mode=pl.Buffered(1))
    stacked = lambda shape: pl.BlockSpec((None,) + tuple(shape[1:]), lambda b, s: (layer, 0, 0),
                                         pipeline_mode=pl.Buffered(1))
    act = lambda width: jax.ShapeDtypeStruct((batch, seq, width), BF16)
    return pl.pallas_call(
        _inproj_kernel,
        grid=(batch, seq // tm),
        in_specs=[
            tok(d), stacked(gain.shape), mod_chunk(0), mod_chunk(1),
            stacked(w_f.shape), stacked(w_gt.shape), stacked(w_d.shape), stacked(b_f.shape),
            const((1, FOX_WIDTH)), const((1, FOX_WIDTH)), const((1, DIFF_WIDTH)), const((1, DIFF_WIDTH)),
        ],
        out_specs=[
            tok(FOX_WIDTH), tok(FOX_WIDTH), tok(FOX_WIDTH),
            pl.BlockSpec((None, FOX_HEADS, tm), lambda b, s: (b, 0, s)),
            tok(DIFF_WIDTH), tok(DIFF_WIDTH), tok(DIFF_WIDTH),
        ],
        out_shape=[
            act(FOX_WIDTH), act(FOX_WIDTH), act(FOX_WIDTH),
            jax.ShapeDtypeStruct((batch, FOX_HEADS, seq), F32),
            act(DIFF_WIDTH), act(DIFF_WIDTH), act(DIFF_WIDTH),
        ],
        scratch_shapes=[pltpu.VMEM((BF16_SUBLANES, LANES), F32)],
        compiler_params=pltpu.CompilerParams(
            dimension_semantics=("arbitrary", "arbitrary"), vmem_limit_bytes=VMEM_LIMIT),
        name="inproj",
    )(x, gain, mod, mod, w_f, w_gt, w_d, b_f, fqg, fkg, dqg, dkg)


def _causal(s):
    row = lax.broadcasted_iota(jnp.int32, s.shape, 0)
    col = lax.broadcasted_iota(jnp.int32, s.shape, 1)
    return jnp.where(col <= row, s, NEG)


def _split_halves(q):
    lane = lax.broadcasted_iota(jnp.int32, q.shape, 1)
    lo = lane < HEAD_DIM
    zero = jnp.zeros_like(q)
    return lo, jnp.where(lo, q, zero), jnp.where(lo, zero, q)


def _flash(qs, k_ref, v_ref, bias_fn, qi, t):
    row0 = qi * t
    half = t // 2
    steps = [(0, key0, t, 0) for key0 in range(0, row0, t)]
    steps += [(0, row0, half, half), (half, row0 + half, half, half)]
    m2 = [jnp.full((t, 1), NEG, F32)] * 2
    acc2 = [None, None]
    for r0, key0, width, tri in steps:
        keys = slice(key0, key0 + width)
        k = k_ref[keys, :]
        v = v_ref[keys, :]
        v_ones = jnp.concatenate([v, jnp.ones_like(v)], axis=1)
        rows = t - r0
        s2 = lax.dot_general(jnp.concatenate([q[r0:] for q in qs], axis=0), k, _NT,
                             preferred_element_type=F32)
        ps, alphas = [], []
        for n, bias in enumerate(bias_fn(key0, width)):
            s = s2[n * rows:(n + 1) * rows] + bias
            if tri:
                s = jnp.concatenate([_causal(s[:tri]), s[tri:]], axis=0) if tri < s.shape[0] else _causal(s)
            m_old = m2[n][r0:]
            m_new = jnp.maximum(m_old, jnp.max(s, axis=-1, keepdims=True))
            ps.append(jnp.exp2(s - m_new).astype(BF16))
            alphas.append(jnp.exp2(m_old - m_new))
            m2[n] = m_new if r0 == 0 else jnp.concatenate([m2[n][:r0], m_new], axis=0)
        pv2 = _dot(jnp.concatenate(ps, axis=0), v_ones)
        for n in range(2):
            pv = pv2[n * rows:(n + 1) * rows]
            acc = pv if acc2[n] is None else acc2[n][r0:] * alphas[n] + pv
            acc2[n] = acc if r0 == 0 else jnp.concatenate([acc2[n][:r0], acc], axis=0)
    return acc2


def _fox_kernel(q_ref, k_ref, v_ref, kb_ref, o_ref):
    t = ATTN_TILE

    def bias_fn(start, width):
        return kb_ref[0:1, start:start + width], kb_ref[1:2, start:start + width]

    for qi in range(q_ref.shape[0] // t):
        rows = slice(qi * t, (qi + 1) * t)
        lo, q0, q1 = _split_halves(q_ref[rows, :])
        acc0, acc1 = _flash((q0, q1), k_ref, v_ref, bias_fn, qi, t)
        out = jnp.where(lo, acc0[:, :LANES], acc1[:, :LANES])
        sums = jnp.where(lo, acc0[:, LANES:], acc1[:, LANES:])
        o_ref[rows, :] = (out / sums).astype(o_ref.dtype)


def _diff_kernel(q_ref, k_ref, v_ref, lam_ref, ng_ref, o_ref, *, lam_init):
    head = pl.program_id(1)
    t = ATTN_TILE
    slope = jnp.exp2(-2.0 * (jnp.full((1, 1), head, jnp.int32) + 1).astype(F32))
    lv = lam_ref[...]
    lam = (jnp.exp(jnp.sum(lv[0:1] * lv[1:2], axis=-1, keepdims=True))
           - jnp.exp(jnp.sum(lv[2:3] * lv[3:4], axis=-1, keepdims=True)) + lam_init)

    def bias_fn(start, width):
        pos = (start + lax.broadcasted_iota(jnp.int32, (1, width), 1)).astype(F32)
        kb = (LOG2E * slope) * pos
        return kb, kb

    for qi in range(q_ref.shape[0] // t):
        rows = slice(qi * t, (qi + 1) * t)
        _, q1, q2 = _split_halves(q_ref[rows, :])
        acc1, acc2 = _flash((q1, q2), k_ref, v_ref, bias_fn, qi, t)
        o = acc1[:, :LANES] / acc1[:, LANES:] - lam * (acc2[:, :LANES] / acc2[:, LANES:])
        ms = jnp.mean(o * o, axis=-1, keepdims=True)
        o_ref[rows, :] = ((o * lax.rsqrt(ms + EPS) * ng_ref[...]) * (1.0 - lam_init)).astype(o_ref.dtype)


def _attn_kernel(fq_ref, fk_ref, fv_ref, kb_ref, dq_ref, dk_ref, dv_ref, lam_ref, ng_ref,
                 fox_ref, diff_ref, *, lam_init):
    _fox_kernel(fq_ref, fk_ref, fv_ref, kb_ref, fox_ref)
    _diff_kernel(dq_ref, dk_ref, dv_ref, lam_ref, ng_ref, diff_ref, lam_init=lam_init)


def _attention(fq, fk, fv, kb, dq, dk, dv, lam_params, norm_gain, lam_init):
    batch, seq, width = fq.shape
    groups = width // LANES
    assert dq.shape[2] // LANES == groups
    kb = kb.reshape(batch, groups, 2, seq)
    seq_block = pl.BlockSpec((None, seq, LANES), lambda b, h: (b, 0, h))
    out = jax.ShapeDtypeStruct((batch, seq, width), BF16)
    return pl.pallas_call(
        functools.partial(_attn_kernel, lam_init=lam_init),
        grid=(batch, groups),
        in_specs=[seq_block, seq_block, seq_block,
                  pl.BlockSpec((None, None, 2, seq), lambda b, h: (b, h, 0, 0)),
                  seq_block, seq_block, seq_block,
                  pl.BlockSpec(lam_params.shape, lambda b, h: (0, 0)),
                  pl.BlockSpec(norm_gain.shape, lambda b, h: (0, 0))],
        out_specs=[seq_block, seq_block],
        out_shape=[out, out],
        compiler_params=pltpu.CompilerParams(dimension_semantics=("arbitrary", "arbitrary")),
        name="attention",
    )(fq, fk, fv, kb, dq, dk, dv, lam_params, norm_gain)


def _mlp_kernel(fox_ref, diff_ref, x_ref, wo_ref, g1_ref, ln_ref, sh_ref, sc_ref, g2_ref,
                wg_ref, wu_ref, wd_ref, o_ref):
    mixed = _dot(fox_ref[...], wo_ref[0:FOX_WIDTH, :]) + _dot(diff_ref[...], wo_ref[FOX_WIDTH:, :])
    x1 = x_ref[...] + g1_ref[...] * mixed
    h = _rms_modulate(x1, ln_ref[...], sh_ref[...], sc_ref[...]).astype(BF16)
    d_ff = wg_ref.shape[1]
    tiles = d_ff // MXU_WIDTH
    bounds = [(tiles * c // FFN_CHUNKS) * MXU_WIDTH for c in range(FFN_CHUNKS)] + [d_ff]
    o_ref[...] = x1
    for c in range(FFN_CHUNKS):
        cols = slice(bounds[c], bounds[c + 1])
        gate = _dot(h, wg_ref[:, cols])
        up = _dot(h, wu_ref[:, cols])
        a = ((gate / (1.0 + jnp.exp(-gate))) * up).astype(BF16)
        o_ref[...] += g2_ref[...] * _dot(a, wd_ref[cols, :])


def _mlp(layer, fox, diff, x, w_o, mod, ln_g, w_gate, w_up, w_down):
    batch, seq, d = x.shape
    tm = TOKEN_TILE
    tok = lambda width: pl.BlockSpec((None, tm, width), lambda b, s: (b, s, 0))
    mod_chunk = lambda i: pl.BlockSpec((None, None, None, 1, d), lambda b, s: (layer, b, i, 0, 0))
    stacked = lambda shape: pl.BlockSpec((None,) + tuple(shape[1:]), lambda b, s: (layer, 0, 0),
                                         pipeline_mode=pl.Buffered(1))
    return pl.pallas_call(
        _mlp_kernel,
        grid=(batch, seq // tm),
        in_specs=[
            tok(FOX_WIDTH), tok(DIFF_WIDTH), tok(d), stacked(w_o.shape), mod_chunk(2), stacked(ln_g.shape),
            mod_chunk(3), mod_chunk(4), mod_chunk(5),
            stacked(w_gate.shape), stacked(w_up.shape), stacked(w_down.shape),
        ],
        out_specs=tok(d),
        out_shape=jax.ShapeDtypeStruct((batch, seq, d), F32),
        compiler_params=pltpu.CompilerParams(
            dimension_semantics=("arbitrary", "arbitrary"), vmem_limit_bytes=VMEM_LIMIT),
        name="out_mlp",
    )(fox, diff, x, w_o, mod, ln_g, mod, mod, mod, w_gate, w_up, w_down)


def kernel(x, c, ln1_g, ln2_g, w_ada, b_ada, w_in, b_f, fox_qk_g, diff_qk_g, diff_lam, diff_norm_g,
           w_out, w_gate, w_up, w_down):
    depth = w_in.shape[0]
    batch, _, d = x.shape
    mod = _ada(c, w_ada, b_ada).reshape(depth, batch, MOD_CHUNKS, 1, d)
    fox_cols = 3 * FOX_WIDTH
    w_f = w_in[:, :, :fox_cols].astype(BF16)
    w_d = w_in[:, :, fox_cols + FOX_HEADS:].astype(BF16)
    w_o, w_g, w_u, w_dn = (w.astype(BF16) for w in (w_out, w_gate, w_up, w_down))
    w_gt = jnp.zeros((depth, BF16_SUBLANES, d), BF16).at[:, :FOX_HEADS].set(
        jnp.transpose(w_in[:, :, fox_cols:fox_cols + FOX_HEADS], (0, 2, 1)).astype(BF16))
    b_fp = jnp.zeros((depth, BF16_SUBLANES, 1), F32).at[:, :FOX_HEADS, 0].set(b_f)
    ln1 = ln1_g.reshape(depth, 1, d)
    ln2 = ln2_g.reshape(depth, 1, d)
    for l in range(depth):
        per_head = lambda g, n: jnp.tile(g, n).reshape(1, n * HEAD_DIM)
        fq, fk, fv, kb, dq, dk, dv = _inproj(
            l, x, ln1, mod, w_f, w_gt, w_d, b_fp,
            per_head(fox_qk_g[l, 0], FOX_HEADS), per_head(fox_qk_g[l, 1], FOX_HEADS),
            per_head(diff_qk_g[l, 0], 2 * DIFF_HEADS), per_head(diff_qk_g[l, 1], 2 * DIFF_HEADS))
        lam_init = 0.8 - 0.6 * math.exp(-0.3 * l)
        fox, diff = _attention(fq, fk, fv, kb, dq, dk, dv, diff_lam[l],
                               diff_norm_g[l].reshape(1, 2 * HEAD_DIM), lam_init)
        x = _mlp(l, fox, diff, x, w_o, mod, ln2, w_g, w_u, w_dn)
    return x
```

```python
import functools
import math

import jax
import jax.numpy as jnp
from jax import lax
from jax.experimental import pallas as pl
from jax.experimental.pallas import tpu as pltpu

D_MODEL = 1024
HEAD_DIM = 64
FOX_HEADS = 8
DIFF_HEADS = 4
FOX_WIDTH = FOX_HEADS * HEAD_DIM
DIFF_WIDTH = DIFF_HEADS * 2 * HEAD_DIM
MOD_CHUNKS = 6
EPS = 1e-6

LANES = 128
MXU_WIDTH = 256
BF16_SUBLANES = 16
LOG2E = math.log2(math.e)
QK_SCALE = LOG2E / math.sqrt(HEAD_DIM)
NEG = -1e30

TOKEN_TILE = 1024
CUMSUM_CHUNK = MXU_WIDTH
ATTN_TILE = 512
FFN_CHUNKS = 4
VMEM_LIMIT = 56 * 1024 * 1024

BF16 = jnp.bfloat16
F32 = jnp.float32

_NT = (((1,), (1,)), ((), ()))


def _dot(a, b):
    return jnp.dot(a, b, preferred_element_type=F32)


def _rms_modulate(x, gain, shift, scale):
    ms = jnp.mean(x * x, axis=-1, keepdims=True)
    return (x * lax.rsqrt(ms + EPS)) * (gain * (1.0 + scale)) + shift


def _ada_kernel(c_ref, w_ref, b_ref, o_ref):
    c = c_ref[...]
    cond = c / (1.0 + jnp.exp(-c))
    o_ref[...] = _dot(cond.astype(BF16), w_ref[...].astype(BF16)) + b_ref[...]


def _ada(c, w_ada, b_ada):
    depth, d, n = w_ada.shape
    batch = c.shape[0]
    nb = n // d
    return pl.pallas_call(
        _ada_kernel,
        grid=(depth, nb),
        in_specs=[
            pl.BlockSpec((batch, d), lambda l, j: (0, 0)),
            pl.BlockSpec((None, d, d), lambda l, j: (l, 0, j)),
            pl.BlockSpec((None, 1, d), lambda l, j: (l, 0, j)),
        ],
        out_specs=pl.BlockSpec((None, batch, d), lambda l, j: (l, 0, j)),
        out_shape=jax.ShapeDtypeStruct((depth, batch, n), F32),
        name="ada_mod",
    )(c, w_ada, b_ada.reshape(depth, 1, n))


def _head_rms(u, gain):
    rows, width = u.shape
    lane = lax.broadcasted_iota(jnp.int32, (rows, LANES), 1)
    lo = lane < HEAD_DIM
    outs = []
    for g in range(width // LANES):
        blk = u[:, g * LANES:(g + 1) * LANES]
        sq = blk * blk
        ss_lo = jnp.sum(jnp.where(lo, sq, 0.0), axis=-1, keepdims=True)
        ss_hi = jnp.sum(jnp.where(lo, 0.0, sq), axis=-1, keepdims=True)
        r = jnp.where(lo, lax.rsqrt(ss_lo / HEAD_DIM + EPS), lax.rsqrt(ss_hi / HEAD_DIM + EPS))
        outs.append(blk * r * gain[:, g * LANES:(g + 1) * LANES])
    return jnp.concatenate(outs, axis=-1)


def _inproj_kernel(x_ref, g_ref, sh_ref, sc_ref, wf_ref, wg_ref, wd_ref, bf_ref,
                   fqg_ref, fkg_ref, dqg_ref, dkg_ref,
                   fq_ref, fk_ref, fv_ref, kb_ref, dq_ref, dk_ref, dv_ref, carry_ref):
    tm = x_ref.shape[0]

    @pl.when(pl.program_id(1) == 0)
    def _():
        carry_ref[...] = jnp.zeros_like(carry_ref)

    h = _rms_modulate(x_ref[...], g_ref[...], sh_ref[...], sc_ref[...]).astype(BF16)

    w = FOX_WIDTH
    fq_ref[...] = _head_rms(_dot(h, wf_ref[:, 0:w]), fqg_ref[...] * QK_SCALE).astype(BF16)
    fk_ref[...] = _head_rms(_dot(h, wf_ref[:, w:2 * w]), fkg_ref[...]).astype(BF16)
    fv_ref[...] = _dot(h, wf_ref[:, 2 * w:3 * w]).astype(BF16)
    w = DIFF_WIDTH
    dq_ref[...] = _head_rms(_dot(h, wd_ref[:, 0:w]), dqg_ref[...] * QK_SCALE).astype(BF16)
    dk_ref[...] = _head_rms(_dot(h, wd_ref[:, w:2 * w]), dkg_ref[...]).astype(BF16)
    dv_ref[...] = _dot(h, wd_ref[:, 2 * w:3 * w]).astype(BF16)

    z = lax.dot_general(wg_ref[...], h, _NT, preferred_element_type=F32) + bf_ref[...]
    logf = jnp.minimum(z, 0.0) - jnp.log1p(jnp.exp(-jnp.abs(z)))

    cw = CUMSUM_CHUNK
    src = lax.broadcasted_iota(jnp.int32, (cw, cw), 0)
    dst = lax.broadcasted_iota(jnp.int32, (cw, cw), 1)
    tri = jnp.where(src <= dst, 1.0, 0.0).astype(BF16)
    segs = jnp.concatenate([logf[:, c0:c0 + cw] for c0 in range(0, tm, cw)], axis=0)
    hi = segs.astype(BF16)
    r1 = segs - hi.astype(F32)
    mid = r1.astype(BF16)
    low = (r1 - mid.astype(F32)).astype(BF16)
    parts = _dot(jnp.concatenate([hi, mid, low], axis=0), tri)
    n = segs.shape[0]
    local = parts[0:n] + parts[n:2 * n] + parts[2 * n:3 * n]
    totals = jnp.sum(segs, axis=-1, keepdims=True)
    carry = carry_ref[:, 0:1]
    rows = logf.shape[0]
    for c in range(tm // cw):
        blk = slice(c * rows, (c + 1) * rows)
        cum = carry + local[blk]
        kb_ref[:, c * cw:(c + 1) * cw] = (-LOG2E) * cum[0:FOX_HEADS, :]
        carry = carry + totals[blk]
    carry_ref[...] = jnp.broadcast_to(carry, carry_ref.shape)


def _inproj(layer, x, gain, mod, w_f, w_gt, w_d, b_f, fqg, fkg, dqg, dkg):
    batch, seq, d = x.shape
    tm = TOKEN_TILE
    tok = lambda width: pl.BlockSpec((None, tm, width), lambda b, s: (b, s, 0))
    mod_chunk = lambda i: pl.BlockSpec((None, None, None, 1, d), lambda b, s: (layer, b, i, 0, 0))
    const = lambda shape: pl.BlockSpec(shape, lambda b, s: (0,) * len(shape),
                                       pipeline_mode=pl.Buffered(1))
    stacked = lambda shape: pl.BlockSpec((None,) + tuple(shape[1:]), lambda b, s: (layer, 0, 0),
                                         pipeline_mode=pl.Buffered(1))
    act = lambda width: jax.ShapeDtypeStruct((batch, seq, width), BF16)
    return pl.pallas_call(
        _inproj_kernel,
        grid=(batch, seq // tm),
        in_specs=[
            tok(d), stacked(gain.shape), mod_chunk(0), mod_chunk(1),
            stacked(w_f.shape), stacked(w_gt.shape), stacked(w_d.shape), stacked(b_f.shape),
            const((1, FOX_WIDTH)), const((1, FOX_WIDTH)), const((1, DIFF_WIDTH)), const((1, DIFF_WIDTH)),
        ],
        out_specs=[
            tok(FOX_WIDTH), tok(FOX_WIDTH), tok(FOX_WIDTH),
            pl.BlockSpec((None, FOX_HEADS, tm), lambda b, s: (b, 0, s)),
            tok(DIFF_WIDTH), tok(DIFF_WIDTH), tok(DIFF_WIDTH),
        ],
        out_shape=[
            act(FOX_WIDTH), act(FOX_WIDTH), act(FOX_WIDTH),
            jax.ShapeDtypeStruct((batch, FOX_HEADS, seq), F32),
            act(DIFF_WIDTH), act(DIFF_WIDTH), act(DIFF_WIDTH),
        ],
        scratch_shapes=[pltpu.VMEM((BF16_SUBLANES, LANES), F32)],
        compiler_params=pltpu.CompilerParams(
            dimension_semantics=("arbitrary", "arbitrary"), vmem_limit_bytes=VMEM_LIMIT),
        name="inproj",
    )(x, gain, mod, mod, w_f, w_gt, w_d, b_f, fqg, fkg, dqg, dkg)


def _causal(s):
    row = lax.broadcasted_iota(jnp.int32, s.shape, 0)
    col = lax.broadcasted_iota(jnp.int32, s.shape, 1)
    return jnp.where(col <= row, s, NEG)


def _split_halves(q):
    lane = lax.broadcasted_iota(jnp.int32, q.shape, 1)
    lo = lane < HEAD_DIM
    zero = jnp.zeros_like(q)
    return lo, jnp.where(lo, q, zero), jnp.where(lo, zero, q)


def _flash(qs, k_ref, v_ref, bias_fn, qi, t):
    row0 = qi * t
    half = t // 2
    steps = [(0, key0, t, 0) for key0 in range(0, row0, t)]
    steps += [(0, row0, half, half), (half, row0 + half, half, half)]
    m2 = [jnp.full((t, 1), NEG, F32)] * 2
    acc2 = [None, None]
    for r0, key0, width, tri in steps:
        keys = slice(key0, key0 + width)
        k = k_ref[keys, :]
        v = v_ref[keys, :]
        v_ones = jnp.concatenate([v, jnp.ones_like(v)], axis=1)
        rows = t - r0
        s2 = lax.dot_general(jnp.concatenate([q[r0:] for q in qs], axis=0), k, _NT,
                             preferred_element_type=F32)
        ps, alphas = [], []
        for n, bias in enumerate(bias_fn(key0, width)):
            s = s2[n * rows:(n + 1) * rows] + bias
            if tri:
                s = jnp.concatenate([_causal(s[:tri]), s[tri:]], axis=0) if tri < s.shape[0] else _causal(s)
            m_old = m2[n][r0:]
            m_new = jnp.maximum(m_old, jnp.max(s, axis=-1, keepdims=True))
            ps.append(jnp.exp2(s - m_new).astype(BF16))
            alphas.append(jnp.exp2(m_old - m_new))
            m2[n] = m_new if r0 == 0 else jnp.concatenate([m2[n][:r0], m_new], axis=0)
        pv2 = _dot(jnp.concatenate(ps, axis=0), v_ones)
        for n in range(2):
            pv = pv2[n * rows:(n + 1) * rows]
            acc = pv if acc2[n] is None else acc2[n][r0:] * alphas[n] + pv
            acc2[n] = acc if r0 == 0 else jnp.concatenate([acc2[n][:r0], acc], axis=0)
    return acc2


def _fox_kernel(q_ref, k_ref, v_ref, kb_ref, o_ref):
    t = ATTN_TILE

    def bias_fn(start, width):
        head = 2 * pl.program_id(1)
        return tuple(kb_ref[pl.ds(head + n, 1), start:start + width] for n in range(2))

    for qi in range(q_ref.shape[0] // t):
        rows = slice(qi * t, (qi + 1) * t)
        lo, q0, q1 = _split_halves(q_ref[rows, :])
        acc0, acc1 = _flash((q0, q1), k_ref, v_ref, bias_fn, qi, t)
        out = jnp.where(lo, acc0[:, :LANES], acc1[:, :LANES])
        sums = jnp.where(lo, acc0[:, LANES:], acc1[:, LANES:])
        o_ref[rows, :] = (out / sums).astype(o_ref.dtype)


def _diff_kernel(q_ref, k_ref, v_ref, lam_ref, ng_ref, o_ref, *, lam_init):
    head = pl.program_id(1)
    t = ATTN_TILE
    slope = jnp.exp2(-2.0 * (jnp.full((1, 1), head, jnp.int32) + 1).astype(F32))
    lv = lam_ref[...]
    lam = (jnp.exp(jnp.sum(lv[0:1] * lv[1:2], axis=-1, keepdims=True))
           - jnp.exp(jnp.sum(lv[2:3] * lv[3:4], axis=-1, keepdims=True)) + lam_init)

    def bias_fn(start, width):
        pos = (start + lax.broadcasted_iota(jnp.int32, (1, width), 1)).astype(F32)
        kb = (LOG2E * slope) * pos
        return kb, kb

    for qi in range(q_ref.shape[0] // t):
        rows = slice(qi * t, (qi + 1) * t)
        _, q1, q2 = _split_halves(q_ref[rows, :])
        acc1, acc2 = _flash((q1, q2), k_ref, v_ref, bias_fn, qi, t)
        o = acc1[:, :LANES] / acc1[:, LANES:] - lam * (acc2[:, :LANES] / acc2[:, LANES:])
        ms = jnp.mean(o * o, axis=-1, keepdims=True)
        o_ref[rows, :] = ((o * lax.rsqrt(ms + EPS) * ng_ref[...]) * (1.0 - lam_init)).astype(o_ref.dtype)


def _attn_kernel(fq_ref, fk_ref, fv_ref, kb_ref, dq_ref, dk_ref, dv_ref, lam_ref, ng_ref,
                 fox_ref, diff_ref, *, lam_init):
    _fox_kernel(fq_ref, fk_ref, fv_ref, kb_ref, fox_ref)
    _diff_kernel(dq_ref, dk_ref, dv_ref, lam_ref, ng_ref, diff_ref, lam_init=lam_init)


def _attention(fq, fk, fv, kb, dq, dk, dv, lam_params, norm_gain, lam_init):
    batch, seq, width = fq.shape
    groups = width // LANES
    assert dq.shape[2] // LANES == groups
    seq_block = pl.BlockSpec((None, seq, LANES), lambda b, h: (b, 0, h))
    out = jax.ShapeDtypeStruct((batch, seq, width), BF16)
    return pl.pallas_call(
        functools.partial(_attn_kernel, lam_init=lam_init),
        grid=(batch, groups),
        in_specs=[seq_block, seq_block, seq_block,
                  pl.BlockSpec((None,) + kb.shape[1:], lambda b, h: (b, 0, 0)),
                  seq_block, seq_block, seq_block,
                  pl.BlockSpec(lam_params.shape, lambda b, h: (0, 0)),
                  pl.BlockSpec(norm_gain.shape, lambda b, h: (0, 0))],
        out_specs=[seq_block, seq_block],
        out_shape=[out, out],
        compiler_params=pltpu.CompilerParams(dimension_semantics=("arbitrary", "arbitrary")),
        name="attention",
    )(fq, fk, fv, kb, dq, dk, dv, lam_params, norm_gain)


def _mlp_kernel(fox_ref, diff_ref, x_ref, wo_ref, g1_ref, ln_ref, sh_ref, sc_ref, g2_ref,
                wg_ref, wu_ref, wd_ref, o_ref):
    mixed = _dot(fox_ref[...], wo_ref[0:FOX_WIDTH, :]) + _dot(diff_ref[...], wo_ref[FOX_WIDTH:, :])
    x1 = x_ref[...] + g1_ref[...] * mixed
    h = _rms_modulate(x1, ln_ref[...], sh_ref[...], sc_ref[...]).astype(BF16)
    d_ff = wg_ref.shape[1]
    tiles = d_ff // MXU_WIDTH
    bounds = [(tiles * c // FFN_CHUNKS) * MXU_WIDTH for c in range(FFN_CHUNKS)] + [d_ff]
    o_ref[...] = x1
    for c in range(FFN_CHUNKS):
        cols = slice(bounds[c], bounds[c + 1])
        gate = _dot(h, wg_ref[:, cols])
        up = _dot(h, wu_ref[:, cols])
        a = ((gate / (1.0 + jnp.exp(-gate))) * up).astype(BF16)
        o_ref[...] += g2_ref[...] * _dot(a, wd_ref[cols, :])


def _mlp(layer, fox, diff, x, w_o, mod, ln_g, w_gate, w_up, w_down):
    batch, seq, d = x.shape
    tm = TOKEN_TILE
    tok = lambda width: pl.BlockSpec((None, tm, width), lambda b, s: (b, s, 0))
    mod_chunk = lambda i: pl.BlockSpec((None, None, None, 1, d), lambda b, s: (layer, b, i, 0, 0))
    stacked = lambda shape: pl.BlockSpec((None,) + tuple(shape[1:]), lambda b, s: (layer, 0, 0),
                                         pipeline_mode=pl.Buffered(1))
    return pl.pallas_call(
        _mlp_kernel,
        grid=(batch, seq // tm),
        in_specs=[
            tok(FOX_WIDTH), tok(DIFF_WIDTH), tok(d), stacked(w_o.shape), mod_chunk(2), stacked(ln_g.shape),
            mod_chunk(3), mod_chunk(4), mod_chunk(5),
            stacked(w_gate.shape), stacked(w_up.shape), stacked(w_down.shape),
        ],
        out_specs=tok(d),
        out_shape=jax.ShapeDtypeStruct((batch, seq, d), F32),
        compiler_params=pltpu.CompilerParams(
            dimension_semantics=("arbitrary", "arbitrary"), vmem_limit_bytes=VMEM_LIMIT),
        name="out_mlp",
    )(fox, diff, x, w_o, mod, ln_g, mod, mod, mod, w_gate, w_up, w_down)


def kernel(x, c, ln1_g, ln2_g, w_ada, b_ada, w_in, b_f, fox_qk_g, diff_qk_g, diff_lam, diff_norm_g,
           w_out, w_gate, w_up, w_down):
    depth = w_in.shape[0]
    batch, _, d = x.shape
    mod = _ada(c, w_ada, b_ada).reshape(depth, batch, MOD_CHUNKS, 1, d)
    fox_cols = 3 * FOX_WIDTH
    w_f = w_in[:, :, :fox_cols].astype(BF16)
    w_d = w_in[:, :, fox_cols + FOX_HEADS:].astype(BF16)
    w_o, w_g, w_u, w_dn = (w.astype(BF16) for w in (w_out, w_gate, w_up, w_down))
    w_gt = jnp.zeros((depth, BF16_SUBLANES, d), BF16).at[:, :FOX_HEADS].set(
        jnp.transpose(w_in[:, :, fox_cols:fox_cols + FOX_HEADS], (0, 2, 1)).astype(BF16))
    b_fp = jnp.zeros((depth, BF16_SUBLANES, 1), F32).at[:, :FOX_HEADS, 0].set(b_f)
    ln1 = ln1_g.reshape(depth, 1, d)
    ln2 = ln2_g.reshape(depth, 1, d)
    for l in range(depth):
        per_head = lambda g, n: jnp.tile(g, n).reshape(1, n * HEAD_DIM)
        fq, fk, fv, kb, dq, dk, dv = _inproj(
            l, x, ln1, mod, w_f, w_gt, w_d, b_fp,
            per_head(fox_qk_g[l, 0], FOX_HEADS), per_head(fox_qk_g[l, 1], FOX_HEADS),
            per_head(diff_qk_g[l, 0], 2 * DIFF_HEADS), per_head(diff_qk_g[l, 1], 2 * DIFF_HEADS))
        lam_init = 0.8 - 0.6 * math.exp(-0.3 * l)
        fox, diff = _attention(fq, fk, fv, kb, dq, dk, dv, diff_lam[l],
                               diff_norm_g[l].reshape(1, 2 * HEAD_DIM), lam_init)
        x = _mlp(l, fox, diff, x, w_o, mod, ln2, w_g, w_u, w_dn)
    return x
```

```python
import functools
import math

import jax
import jax.numpy as jnp
from jax import lax
from jax.experimental import pallas as pl
from jax.experimental.pallas import tpu as pltpu

D_MODEL = 1024
HEAD_DIM = 64
FOX_HEADS = 8
DIFF_HEADS = 4
FOX_WIDTH = FOX_HEADS * HEAD_DIM
DIFF_WIDTH = DIFF_HEADS * 2 * HEAD_DIM
MOD_CHUNKS = 6
EPS = 1e-6

LANES = 128
MXU_WIDTH = 256
BF16_SUBLANES = 16
LOG2E = math.log2(math.e)
QK_SCALE = LOG2E / math.sqrt(HEAD_DIM)
NEG = -1e30

TOKEN_TILE = 1024
CUMSUM_CHUNK = MXU_WIDTH
ATTN_TILE = 512
FFN_CHUNKS = 4
VMEM_LIMIT = 56 * 1024 * 1024

BF16 = jnp.bfloat16
F32 = jnp.float32

_NT = (((1,), (1,)), ((), ()))


def _dot(a, b):
    return jnp.dot(a, b, preferred_element_type=F32)


def _rms_modulate(x, gain, shift, scale):
    ms = jnp.mean(x * x, axis=-1, keepdims=True)
    return (x * lax.rsqrt(ms + EPS)) * (gain * (1.0 + scale)) + shift


def _ada_kernel(c_ref, w_ref, b_ref, o_ref):
    c = c_ref[...]
    cond = c / (1.0 + jnp.exp(-c))
    o_ref[...] = _dot(cond.astype(BF16), w_ref[...].astype(BF16)) + b_ref[...]


def _ada(c, w_ada, b_ada):
    depth, d, n = w_ada.shape
    batch = c.shape[0]
    nb = n // d
    return pl.pallas_call(
        _ada_kernel,
        grid=(depth, nb),
        in_specs=[
            pl.BlockSpec((batch, d), lambda l, j: (0, 0)),
            pl.BlockSpec((None, d, d), lambda l, j: (l, 0, j)),
            pl.BlockSpec((None, 1, d), lambda l, j: (l, 0, j)),
        ],
        out_specs=pl.BlockSpec((None, batch, d), lambda l, j: (l, 0, j)),
        out_shape=jax.ShapeDtypeStruct((depth, batch, n), F32),
        name="ada_mod",
    )(c, w_ada, b_ada.reshape(depth, 1, n))


def _head_rms(u, gain):
    rows, width = u.shape
    lane = lax.broadcasted_iota(jnp.int32, (rows, LANES), 1)
    lo = lane < HEAD_DIM
    outs = []
    for g in range(width // LANES):
        blk = u[:, g * LANES:(g + 1) * LANES]
        sq = blk * blk
        ss_lo = jnp.sum(jnp.where(lo, sq, 0.0), axis=-1, keepdims=True)
        ss_hi = jnp.sum(jnp.where(lo, 0.0, sq), axis=-1, keepdims=True)
        r = jnp.where(lo, lax.rsqrt(ss_lo / HEAD_DIM + EPS), lax.rsqrt(ss_hi / HEAD_DIM + EPS))
        outs.append(blk * r * gain[:, g * LANES:(g + 1) * LANES])
    return jnp.concatenate(outs, axis=-1)


def _inproj_kernel(x_ref, g_ref, sh_ref, sc_ref, wf_ref, wg_ref, wd_ref, bf_ref,
                   fqg_ref, fkg_ref, dqg_ref, dkg_ref,
                   fq_ref, fk_ref, fv_ref, kb_ref, dq_ref, dk_ref, dv_ref, carry_ref):
    tm = x_ref.shape[0]

    @pl.when(pl.program_id(1) == 0)
    def _():
        carry_ref[...] = jnp.zeros_like(carry_ref)

    h = _rms_modulate(x_ref[...], g_ref[...], sh_ref[...], sc_ref[...]).astype(BF16)

    w = FOX_WIDTH
    fq_ref[...] = _head_rms(_dot(h, wf_ref[:, 0:w]), fqg_ref[...] * QK_SCALE).astype(BF16)
    fk_ref[...] = _head_rms(_dot(h, wf_ref[:, w:2 * w]), fkg_ref[...]).astype(BF16)
    fv_ref[...] = _dot(h, wf_ref[:, 2 * w:3 * w]).astype(BF16)
    w = DIFF_WIDTH
    dq_ref[...] = _head_rms(_dot(h, wd_ref[:, 0:w]), dqg_ref[...] * QK_SCALE).astype(BF16)
    dk_ref[...] = _head_rms(_dot(h, wd_ref[:, w:2 * w]), dkg_ref[...]).astype(BF16)
    dv_ref[...] = _dot(h, wd_ref[:, 2 * w:3 * w]).astype(BF16)

    z = lax.dot_general(wg_ref[...], h, _NT, preferred_element_type=F32) + bf_ref[...]
    logf = jnp.minimum(z, 0.0) - jnp.log1p(jnp.exp(-jnp.abs(z)))

    cw = CUMSUM_CHUNK
    src = lax.broadcasted_iota(jnp.int32, (cw, cw), 0)
    dst = lax.broadcasted_iota(jnp.int32, (cw, cw), 1)
    tri = jnp.where(src <= dst, 1.0, 0.0).astype(BF16)
    segs = jnp.concatenate([logf[:, c0:c0 + cw] for c0 in range(0, tm, cw)], axis=0)
    hi = segs.astype(BF16)
    r1 = segs - hi.astype(F32)
    mid = r1.astype(BF16)
    low = (r1 - mid.astype(F32)).astype(BF16)
    parts = _dot(jnp.concatenate([hi, mid, low], axis=0), tri)
    n = segs.shape[0]
    local = parts[0:n] + parts[n:2 * n] + parts[2 * n:3 * n]
    totals = jnp.sum(segs, axis=-1, keepdims=True)
    carry = carry_ref[:, 0:1]
    rows = logf.shape[0]
    for c in range(tm // cw):
        blk = slice(c * rows, (c + 1) * rows)
        cum = carry + local[blk]
        kb_ref[:, c * cw:(c + 1) * cw] = (-LOG2E) * cum[0:FOX_HEADS, :]
        carry = carry + totals[blk]
    carry_ref[...] = jnp.broadcast_to(carry, carry_ref.shape)


def _inproj(layer, x, gain, mod, w_f, w_gt, w_d, b_f, fqg, fkg, dqg, dkg):
    batch, seq, d = x.shape
    tm = TOKEN_TILE
    tok = lambda width: pl.BlockSpec((None, tm, width), lambda b, s: (b, s, 0))
    mod_chunk = lambda i: pl.BlockSpec((None, None, None, 1, d), lambda b, s: (layer, b, i, 0, 0))
    const = lambda shape: pl.BlockSpec(shape, lambda b, s: (0,) * len(shape),
                                       pipeline_mode=pl.Buffered(1))
    stacked = lambda shape: pl.BlockSpec((None,) + tuple(shape[1:]), lambda b, s: (layer, 0, 0),
                                         pipeline_mode=pl.Buffered(1))
    act = lambda width: jax.ShapeDtypeStruct((batch, seq, width), BF16)
    return pl.pallas_call(
        _inproj_kernel,
        grid=(batch, seq // tm),
        in_specs=[
            tok(d), stacked(gain.shape), mod_chunk(0), mod_chunk(1),
            stacked(w_f.shape), stacked(w_gt.shape), stacked(w_d.shape), stacked(b_f.shape),
            const((1, FOX_WIDTH)), const((1, FOX_WIDTH)), const((1, DIFF_WIDTH)), const((1, DIFF_WIDTH)),
        ],
        out_specs=[
            tok(FOX_WIDTH), tok(FOX_WIDTH), tok(FOX_WIDTH),
            pl.BlockSpec((None, FOX_HEADS, tm), lambda b, s: (b, 0, s)),
            tok(DIFF_WIDTH), tok(DIFF_WIDTH), tok(DIFF_WIDTH),
        ],
        out_shape=[
            act(FOX_WIDTH), act(FOX_WIDTH), act(FOX_WIDTH),
            jax.ShapeDtypeStruct((batch, FOX_HEADS, seq), F32),
            act(DIFF_WIDTH), act(DIFF_WIDTH), act(DIFF_WIDTH),
        ],
        scratch_shapes=[pltpu.VMEM((BF16_SUBLANES, LANES), F32)],
        compiler_params=pltpu.CompilerParams(
            dimension_semantics=("arbitrary", "arbitrary"), vmem_limit_bytes=VMEM_LIMIT),
        name="inproj",
    )(x, gain, mod, mod, w_f, w_gt, w_d, b_f, fqg, fkg, dqg, dkg)


def _causal(s):
    row = lax.broadcasted_iota(jnp.int32, s.shape, 0)
    col = lax.broadcasted_iota(jnp.int32, s.shape, 1)
    return jnp.where(col <= row, s, NEG)


def _split_halves(q):
    lane = lax.broadcasted_iota(jnp.int32, q.shape, 1)
    lo = lane < HEAD_DIM
    zero = jnp.zeros_like(q)
    return lo, jnp.where(lo, q, zero), jnp.where(lo, zero, q)


def _flash(qs, k_ref, v_ref, bias_fn, qi, t):
    row0 = qi * t
    half = t // 2
    steps = [(0, key0, t, 0) for key0 in range(0, row0, t)]
    steps += [(0, row0, half, half), (half, row0 + half, half, half)]
    m2 = [jnp.full((t, 1), NEG, F32)] * 2
    acc2 = [None, None]
    for r0, key0, width, tri in steps:
        keys = slice(key0, key0 + width)
        k = k_ref[keys, :]
        v = v_ref[keys, :]
        v_ones = jnp.concatenate([v, jnp.ones_like(v)], axis=1)
        rows = t - r0
        s2 = lax.dot_general(jnp.concatenate([q[r0:] for q in qs], axis=0), k, _NT,
                             preferred_element_type=F32)
        ps, alphas = [], []
        for n, bias in enumerate(bias_fn(key0, width)):
            s = s2[n * rows:(n + 1) * rows] + bias
            if tri:
                s = jnp.concatenate([_causal(s[:tri]), s[tri:]], axis=0) if tri < s.shape[0] else _causal(s)
            m_old = m2[n][r0:]
            m_new = jnp.maximum(m_old, jnp.max(s, axis=-1, keepdims=True))
            ps.append(jnp.exp2(s - m_new).astype(BF16))
            alphas.append(jnp.exp2(m_old - m_new))
            m2[n] = m_new if r0 == 0 else jnp.concatenate([m2[n][:r0], m_new], axis=0)
        pv2 = _dot(jnp.concatenate(ps, axis=0), v_ones)
        for n in range(2):
            pv = pv2[n * rows:(n + 1) * rows]
            acc = pv if acc2[n] is None else acc2[n][r0:] * alphas[n] + pv
            acc2[n] = acc if r0 == 0 else jnp.concatenate([acc2[n][:r0], acc], axis=0)
    return acc2


def _fox_kernel(q_ref, k_ref, v_ref, kb_ref, o_ref):
    t = ATTN_TILE

    def bias_fn(start, width):
        head = 2 * pl.program_id(1)
        return tuple(kb_ref[pl.ds(head + n, 1), start:start + width] for n in range(2))

    for qi in range(q_ref.shape[0] // t):
        rows = slice(qi * t, (qi + 1) * t)
        lo, q0, q1 = _split_halves(q_ref[rows, :])
        acc0, acc1 = _flash((q0, q1), k_ref, v_ref, bias_fn, qi, t)
        out = jnp.where(lo, acc0[:, :LANES], acc1[:, :LANES])
        sums = jnp.where(lo, acc0[:, LANES:], acc1[:, LANES:])
        o_ref[rows, :] = (out / sums).astype(o_ref.dtype)


def _diff_kernel(q_ref, k_ref, v_ref, lam_ref, ng_ref, o_ref, *, lam_init):
    head = pl.program_id(1)
    t = ATTN_TILE
    slope = jnp.exp2(-2.0 * (jnp.full((1, 1), head, jnp.int32) + 1).astype(F32))
    lv = lam_ref[...]
    lam = (jnp.exp(jnp.sum(lv[0:1] * lv[1:2], axis=-1, keepdims=True))
           - jnp.exp(jnp.sum(lv[2:3] * lv[3:4], axis=-1, keepdims=True)) + lam_init)

    def bias_fn(start, width):
        pos = (start + lax.broadcasted_iota(jnp.int32, (1, width), 1)).astype(F32)
        kb = (LOG2E * slope) * pos
        return kb, kb

    for qi in range(q_ref.shape[0] // t):
        rows = slice(qi * t, (qi + 1) * t)
        _, q1, q2 = _split_halves(q_ref[rows, :])
        acc1, acc2 = _flash((q1, q2), k_ref, v_ref, bias_fn, qi, t)
        o = acc1[:, :LANES] / acc1[:, LANES:] - lam * (acc2[:, :LANES] / acc2[:, LANES:])
        ms = jnp.mean(o * o, axis=-1, keepdims=True)
        o_ref[rows, :] = ((o * lax.rsqrt(ms + EPS) * ng_ref[...]) * (1.0 - lam_init)).astype(o_ref.dtype)


def _attn_kernel(fq_ref, fk_ref, fv_ref, kb_ref, dq_ref, dk_ref, dv_ref, lam_ref, ng_ref,
                 fox_ref, diff_ref, *, lam_init):
    _fox_kernel(fq_ref, fk_ref, fv_ref, kb_ref, fox_ref)
    _diff_kernel(dq_ref, dk_ref, dv_ref, lam_ref, ng_ref, diff_ref, lam_init=lam_init)


def _attention(fq, fk, fv, kb, dq, dk, dv, lam_params, norm_gain, lam_init):
    batch, seq, width = fq.shape
    groups = width // LANES
    assert dq.shape[2] // LANES == groups
    seq_block = pl.BlockSpec((None, seq, LANES), lambda b, h: (b, 0, h))
    out = jax.ShapeDtypeStruct((batch, seq, width), BF16)
    return pl.pallas_call(
        functools.partial(_attn_kernel, lam_init=lam_init),
        grid=(batch, groups),
        in_specs=[seq_block, seq_block, seq_block,
                  pl.BlockSpec((None,) + kb.shape[1:], lambda b, h: (b, 0, 0)),
                  seq_block, seq_block, seq_block,
                  pl.BlockSpec(lam_params.shape, lambda b, h: (0, 0)),
                  pl.BlockSpec(norm_gain.shape, lambda b, h: (0, 0))],
        out_specs=[seq_block, seq_block],
        out_shape=[out, out],
        compiler_params=pltpu.CompilerParams(dimension_semantics=("arbitrary", "arbitrary")),
        name="attention",
    )(fq, fk, fv, kb, dq, dk, dv, lam_params, norm_gain)


def _mlp_kernel(fox_ref, diff_ref, x_ref, wo_ref, g1_ref, ln_ref, sh_ref, sc_ref, g2_ref,
                wg_ref, wu_ref, wd_ref, o_ref):
    mixed = _dot(jnp.concatenate([fox_ref[...], diff_ref[...]], axis=1), wo_ref[...])
    x1 = x_ref[...] + g1_ref[...] * mixed
    h = _rms_modulate(x1, ln_ref[...], sh_ref[...], sc_ref[...]).astype(BF16)
    d_ff = wg_ref.shape[1]
    tiles = d_ff // MXU_WIDTH
    bounds = [(tiles * c // FFN_CHUNKS) * MXU_WIDTH for c in range(FFN_CHUNKS)] + [d_ff]
    o_ref[...] = x1
    for c in range(FFN_CHUNKS):
        cols = slice(bounds[c], bounds[c + 1])
        gate = _dot(h, wg_ref[:, cols])
        up = _dot(h, wu_ref[:, cols])
        a = ((gate / (1.0 + jnp.exp(-gate))) * up).astype(BF16)
        o_ref[...] += g2_ref[...] * _dot(a, wd_ref[cols, :])


def _mlp(layer, fox, diff, x, w_o, mod, ln_g, w_gate, w_up, w_down):
    batch, seq, d = x.shape
    tm = TOKEN_TILE
    tok = lambda width: pl.BlockSpec((None, tm, width), lambda b, s: (b, s, 0))
    mod_chunk = lambda i: pl.BlockSpec((None, None, None, 1, d), lambda b, s: (layer, b, i, 0, 0))
    stacked = lambda shape: pl.BlockSpec((None,) + tuple(shape[1:]), lambda b, s: (layer, 0, 0),
                                         pipeline_mode=pl.Buffered(1))
    return pl.pallas_call(
        _mlp_kernel,
        grid=(batch, seq // tm),
        in_specs=[
            tok(FOX_WIDTH), tok(DIFF_WIDTH), tok(d), stacked(w_o.shape), mod_chunk(2), stacked(ln_g.shape),
            mod_chunk(3), mod_chunk(4), mod_chunk(5),
            stacked(w_gate.shape), stacked(w_up.shape), stacked(w_down.shape),
        ],
        out_specs=tok(d),
        out_shape=jax.ShapeDtypeStruct((batch, seq, d), F32),
        compiler_params=pltpu.CompilerParams(
            dimension_semantics=("arbitrary", "arbitrary"), vmem_limit_bytes=VMEM_LIMIT),
        name="out_mlp",
    )(fox, diff, x, w_o, mod, ln_g, mod, mod, mod, w_gate, w_up, w_down)


def kernel(x, c, ln1_g, ln2_g, w_ada, b_ada, w_in, b_f, fox_qk_g, diff_qk_g, diff_lam, diff_norm_g,
           w_out, w_gate, w_up, w_down):
    depth = w_in.shape[0]
    batch, _, d = x.shape
    mod = _ada(c, w_ada, b_ada).reshape(depth, batch, MOD_CHUNKS, 1, d)
    fox_cols = 3 * FOX_WIDTH
    w_f = w_in[:, :, :fox_cols].astype(BF16)
    w_d = w_in[:, :, fox_cols + FOX_HEADS:].astype(BF16)
    w_o, w_g, w_u, w_dn = (w.astype(BF16) for w in (w_out, w_gate, w_up, w_down))
    w_gt = jnp.zeros((depth, BF16_SUBLANES, d), BF16).at[:, :FOX_HEADS].set(
        jnp.transpose(w_in[:, :, fox_cols:fox_cols + FOX_HEADS], (0, 2, 1)).astype(BF16))
    b_fp = jnp.zeros((depth, BF16_SUBLANES, 1), F32).at[:, :FOX_HEADS, 0].set(b_f)
    ln1 = ln1_g.reshape(depth, 1, d)
    ln2 = ln2_g.reshape(depth, 1, d)
    for l in range(depth):
        per_head = lambda g, n: jnp.tile(g, n).reshape(1, n * HEAD_DIM)
        fq, fk, fv, kb, dq, dk, dv = _inproj(
            l, x, ln1, mod, w_f, w_gt, w_d, b_fp,
            per_head(fox_qk_g[l, 0], FOX_HEADS), per_head(fox_qk_g[l, 1], FOX_HEADS),
            per_head(diff_qk_g[l, 0], 2 * DIFF_HEADS), per_head(diff_qk_g[l, 1], 2 * DIFF_HEADS))
        lam_init = 0.8 - 0.6 * math.exp(-0.3 * l)
        fox, diff = _attention(fq, fk, fv, kb, dq, dk, dv, diff_lam[l],
                               diff_norm_g[l].reshape(1, 2 * HEAD_DIM), lam_init)
        x = _mlp(l, fox, diff, x, w_o, mod, ln2, w_g, w_u, w_dn)
    return x
```

```python
import functools
import math

import jax
import jax.numpy as jnp
from jax import lax
from jax.experimental import pallas as pl
from jax.experimental.pallas import tpu as pltpu

D_MODEL = 1024
HEAD_DIM = 64
FOX_HEADS = 8
DIFF_HEADS = 4
FOX_WIDTH = FOX_HEADS * HEAD_DIM
DIFF_WIDTH = DIFF_HEADS * 2 * HEAD_DIM
MOD_CHUNKS = 6
EPS = 1e-6

LANES = 128
MXU_WIDTH = 256
BF16_SUBLANES = 16
LOG2E = math.log2(math.e)
QK_SCALE = LOG2E / math.sqrt(HEAD_DIM)
NEG = -1e30

TOKEN_TILE = 1024
CUMSUM_CHUNK = MXU_WIDTH
ATTN_TILE = 512
FFN_CHUNKS = 4
VMEM_LIMIT = 56 * 1024 * 1024

BF16 = jnp.bfloat16
F32 = jnp.float32

_NT = (((1,), (1,)), ((), ()))


def _dot(a, b):
    return jnp.dot(a, b, preferred_element_type=F32)


def _rms_modulate(x, gain, shift, scale):
    ms = jnp.mean(x * x, axis=-1, keepdims=True)
    return (x * lax.rsqrt(ms + EPS)) * (gain * (1.0 + scale)) + shift


def _ada_kernel(c_ref, w_ref, b_ref, o_ref):
    c = c_ref[...]
    cond = c / (1.0 + jnp.exp(-c))
    o_ref[...] = _dot(cond.astype(BF16), w_ref[...].astype(BF16)) + b_ref[...]


def _ada(c, w_ada, b_ada):
    depth, d, n = w_ada.shape
    batch = c.shape[0]
    nb = n // d
    return pl.pallas_call(
        _ada_kernel,
        grid=(depth, nb),
        in_specs=[
            pl.BlockSpec((batch, d), lambda l, j: (0, 0)),
            pl.BlockSpec((None, d, d), lambda l, j: (l, 0, j)),
            pl.BlockSpec((None, 1, d), lambda l, j: (l, 0, j)),
        ],
        out_specs=pl.BlockSpec((None, batch, d), lambda l, j: (l, 0, j)),
        out_shape=jax.ShapeDtypeStruct((depth, batch, n), F32),
        name="ada_mod",
    )(c, w_ada, b_ada.reshape(depth, 1, n))


def _head_rms(u, gain):
    rows, width = u.shape
    lane = lax.broadcasted_iota(jnp.int32, (rows, LANES), 1)
    lo = lane < HEAD_DIM
    outs = []
    for g in range(width // LANES):
        blk = u[:, g * LANES:(g + 1) * LANES]
        sq = blk * blk
        ss_lo = jnp.sum(jnp.where(lo, sq, 0.0), axis=-1, keepdims=True)
        ss_hi = jnp.sum(jnp.where(lo, 0.0, sq), axis=-1, keepdims=True)
        r = jnp.where(lo, lax.rsqrt(ss_lo / HEAD_DIM + EPS), lax.rsqrt(ss_hi / HEAD_DIM + EPS))
        outs.append(blk * r * gain[:, g * LANES:(g + 1) * LANES])
    return jnp.concatenate(outs, axis=-1)


def _inproj_kernel(x_ref, g_ref, sh_ref, sc_ref, wf_ref, wg_ref, wd_ref, bf_ref,
                   fqg_ref, fkg_ref, dqg_ref, dkg_ref,
                   fq_ref, fk_ref, fv_ref, kb_ref, dq_ref, dk_ref, dv_ref, carry_ref):
    tm = x_ref.shape[0]

    @pl.when(pl.program_id(1) == 0)
    def _():
        carry_ref[...] = jnp.zeros_like(carry_ref)

    h = _rms_modulate(x_ref[...], g_ref[...], sh_ref[...], sc_ref[...]).astype(BF16)

    w = FOX_WIDTH
    fq_ref[...] = _head_rms(_dot(h, wf_ref[:, 0:w]), fqg_ref[...] * QK_SCALE).astype(BF16)
    fk_ref[...] = _head_rms(_dot(h, wf_ref[:, w:2 * w]), fkg_ref[...]).astype(BF16)
    fv_ref[...] = _dot(h, wf_ref[:, 2 * w:3 * w]).astype(BF16)
    w = DIFF_WIDTH
    dq_ref[...] = _head_rms(_dot(h, wd_ref[:, 0:w]), dqg_ref[...] * QK_SCALE).astype(BF16)
    dk_ref[...] = _head_rms(_dot(h, wd_ref[:, w:2 * w]), dkg_ref[...]).astype(BF16)
    dv_ref[...] = _dot(h, wd_ref[:, 2 * w:3 * w]).astype(BF16)

    z = lax.dot_general(wg_ref[...], h, _NT, preferred_element_type=F32) + bf_ref[...]
    logf = jnp.minimum(z, 0.0) - jnp.log1p(jnp.exp(-jnp.abs(z)))

    cw = CUMSUM_CHUNK
    src = lax.broadcasted_iota(jnp.int32, (cw, cw), 0)
    dst = lax.broadcasted_iota(jnp.int32, (cw, cw), 1)
    tri = jnp.where(src <= dst, 1.0, 0.0).astype(BF16)
    segs = jnp.concatenate([logf[:, c0:c0 + cw] for c0 in range(0, tm, cw)], axis=0)
    hi = segs.astype(BF16)
    r1 = segs - hi.astype(F32)
    mid = r1.astype(BF16)
    low = (r1 - mid.astype(F32)).astype(BF16)
    parts = _dot(jnp.concatenate([hi, mid, low], axis=0), tri)
    n = segs.shape[0]
    local = parts[0:n] + parts[n:2 * n] + parts[2 * n:3 * n]
    totals = jnp.sum(segs, axis=-1, keepdims=True)
    carry = carry_ref[:, 0:1]
    rows = logf.shape[0]
    for c in range(tm // cw):
        blk = slice(c * rows, (c + 1) * rows)
        cum = carry + local[blk]
        kb_ref[:, c * cw:(c + 1) * cw] = (-LOG2E) * cum[0:FOX_HEADS, :]
        carry = carry + totals[blk]
    carry_ref[...] = jnp.broadcast_to(carry, carry_ref.shape)


def _inproj(layer, x, gain, mod, w_f, w_gt, w_d, b_f, fqg, fkg, dqg, dkg):
    batch, seq, d = x.shape
    tm = TOKEN_TILE
    tok = lambda width: pl.BlockSpec((None, tm, width), lambda b, s: (b, s, 0))
    mod_chunk = lambda i: pl.BlockSpec((None, None, None, 1, d), lambda b, s: (layer, b, i, 0, 0))
    const = lambda shape: pl.BlockSpec(shape, lambda b, s: (0,) * len(shape),
                                       pipeline_mode=pl.Buffered(1))
    stacked = lambda shape: pl.BlockSpec((None,) + tuple(shape[1:]), lambda b, s: (layer, 0, 0),
                                         pipeline_mode=pl.Buffered(1))
    act = lambda width: jax.ShapeDtypeStruct((batch, seq, width), BF16)
    return pl.pallas_call(
        _inproj_kernel,
        grid=(batch, seq // tm),
        in_specs=[
            tok(d), stacked(gain.shape), mod_chunk(0), mod_chunk(1),
            stacked(w_f.shape), stacked(w_gt.shape), stacked(w_d.shape), stacked(b_f.shape),
            const((1, FOX_WIDTH)), const((1, FOX_WIDTH)), const((1, DIFF_WIDTH)), const((1, DIFF_WIDTH)),
        ],
        out_specs=[
            tok(FOX_WIDTH), tok(FOX_WIDTH), tok(FOX_WIDTH),
            pl.BlockSpec((None, FOX_HEADS, tm), lambda b, s: (b, 0, s)),
            tok(DIFF_WIDTH), tok(DIFF_WIDTH), tok(DIFF_WIDTH),
        ],
        out_shape=[
            act(FOX_WIDTH), act(FOX_WIDTH), act(FOX_WIDTH),
            jax.ShapeDtypeStruct((batch, FOX_HEADS, seq), F32),
            act(DIFF_WIDTH), act(DIFF_WIDTH), act(DIFF_WIDTH),
        ],
        scratch_shapes=[pltpu.VMEM((BF16_SUBLANES, LANES), F32)],
        compiler_params=pltpu.CompilerParams(
            dimension_semantics=("arbitrary", "arbitrary"), vmem_limit_bytes=VMEM_LIMIT),
        name="inproj",
    )(x, gain, mod, mod, w_f, w_gt, w_d, b_f, fqg, fkg, dqg, dkg)


def _causal(s):
    row = lax.broadcasted_iota(jnp.int32, s.shape, 0)
    col = lax.broadcasted_iota(jnp.int32, s.shape, 1)
    return jnp.where(col <= row, s, NEG)


def _split_halves(q):
    lane = lax.broadcasted_iota(jnp.int32, q.shape, 1)
    lo = lane < HEAD_DIM
    zero = jnp.zeros_like(q)
    return lo, jnp.where(lo, q, zero), jnp.where(lo, zero, q)


def _flash(qs, k_ref, v_ref, bias_fn, qi, t):
    row0 = qi * t
    half = t // 2
    steps = [(0, key0, t, 0) for key0 in range(0, row0, t)]
    steps += [(0, row0, half, half), (half, row0 + half, half, half)]
    m2 = [jnp.full((t, 1), NEG, F32)] * 2
    acc2 = [None, None]
    for r0, key0, width, tri in steps:
        keys = slice(key0, key0 + width)
        k = k_ref[keys, :]
        v = v_ref[keys, :]
        v_ones = jnp.concatenate([v, jnp.ones_like(v)], axis=1)
        rows = t - r0
        s2 = lax.dot_general(jnp.concatenate([q[r0:] for q in qs], axis=0), k, _NT,
                             preferred_element_type=F32)
        ps, alphas = [], []
        for n, bias in enumerate(bias_fn(key0, width)):
            s = s2[n * rows:(n + 1) * rows] + bias
            if tri:
                s = jnp.concatenate([_causal(s[:tri]), s[tri:]], axis=0) if tri < s.shape[0] else _causal(s)
            m_old = m2[n][r0:]
            m_new = jnp.maximum(m_old, jnp.max(s, axis=-1, keepdims=True))
            ps.append(jnp.exp2(s - m_new).astype(BF16))
            alphas.append(jnp.exp2(m_old - m_new))
            m2[n] = m_new if r0 == 0 else jnp.concatenate([m2[n][:r0], m_new], axis=0)
        pv2 = _dot(jnp.concatenate(ps, axis=0), v_ones)
        for n in range(2):
            pv = pv2[n * rows:(n + 1) * rows]
            acc = pv if acc2[n] is None else acc2[n][r0:] * alphas[n] + pv
            acc2[n] = acc if r0 == 0 else jnp.concatenate([acc2[n][:r0], acc], axis=0)
    return acc2


def _fox_kernel(q_ref, k_ref, v_ref, kb_ref, o_ref):
    t = ATTN_TILE

    def bias_fn(start, width):
        head = 2 * pl.program_id(1)
        return tuple(kb_ref[pl.ds(head + n, 1), start:start + width] for n in range(2))

    for qi in range(q_ref.shape[0] // t):
        rows = slice(qi * t, (qi + 1) * t)
        lo, q0, q1 = _split_halves(q_ref[rows, :])
        acc0, acc1 = _flash((q0, q1), k_ref, v_ref, bias_fn, qi, t)
        out = jnp.where(lo, acc0[:, :LANES], acc1[:, :LANES])
        sums = jnp.where(lo, acc0[:, LANES:], acc1[:, LANES:])
        o_ref[rows, :] = (out / sums).astype(o_ref.dtype)


def _diff_kernel(q_ref, k_ref, v_ref, lam_ref, ng_ref, o_ref, *, lam_init):
    head = pl.program_id(1)
    t = ATTN_TILE
    slope = jnp.exp2(-2.0 * (jnp.full((1, 1), head, jnp.int32) + 1).astype(F32))
    lv = lam_ref[...]
    lam = (jnp.exp(jnp.sum(lv[0:1] * lv[1:2], axis=-1, keepdims=True))
           - jnp.exp(jnp.sum(lv[2:3] * lv[3:4], axis=-1, keepdims=True)) + lam_init)

    def bias_fn(start, width):
        pos = (start + lax.broadcasted_iota(jnp.int32, (1, width), 1)).astype(F32)
        kb = (LOG2E * slope) * pos
        return kb, kb

    for qi in range(q_ref.shape[0] // t):
        rows = slice(qi * t, (qi + 1) * t)
        _, q1, q2 = _split_halves(q_ref[rows, :])
        acc1, acc2 = _flash((q1, q2), k_ref, v_ref, bias_fn, qi, t)
        o = acc1[:, :LANES] / acc1[:, LANES:] - lam * (acc2[:, :LANES] / acc2[:, LANES:])
        ms = jnp.mean(o * o, axis=-1, keepdims=True)
        o_ref[rows, :] = ((o * lax.rsqrt(ms + EPS) * ng_ref[...]) * (1.0 - lam_init)).astype(o_ref.dtype)


def _attn_kernel(fq_ref, fk_ref, fv_ref, kb_ref, dq_ref, dk_ref, dv_ref, lam_ref, ng_ref,
                 fox_ref, diff_ref, *, lam_init):
    _fox_kernel(fq_ref, fk_ref, fv_ref, kb_ref, fox_ref)
    _diff_kernel(dq_ref, dk_ref, dv_ref, lam_ref, ng_ref, diff_ref, lam_init=lam_init)


def _attention(fq, fk, fv, kb, dq, dk, dv, lam_params, norm_gain, lam_init):
    batch, seq, width = fq.shape
    groups = width // LANES
    assert dq.shape[2] // LANES == groups
    seq_block = pl.BlockSpec((None, seq, LANES), lambda b, h: (b, 0, h))
    out = jax.ShapeDtypeStruct((batch, seq, width), BF16)
    return pl.pallas_call(
        functools.partial(_attn_kernel, lam_init=lam_init),
        grid=(batch, groups),
        in_specs=[seq_block, seq_block, seq_block,
                  pl.BlockSpec((None,) + kb.shape[1:], lambda b, h: (b, 0, 0)),
                  seq_block, seq_block, seq_block,
                  pl.BlockSpec(lam_params.shape, lambda b, h: (0, 0)),
                  pl.BlockSpec(norm_gain.shape, lambda b, h: (0, 0))],
        out_specs=[seq_block, seq_block],
        out_shape=[out, out],
        compiler_params=pltpu.CompilerParams(dimension_semantics=("arbitrary", "arbitrary")),
        name="attention",
    )(fq, fk, fv, kb, dq, dk, dv, lam_params, norm_gain)


def _mlp_kernel(fox_ref, diff_ref, x_ref, wo_ref, g1_ref, ln_ref, sh_ref, sc_ref, g2_ref,
                wg_ref, wu_ref, wd_ref, o_ref):
    half = x_ref.shape[0] // 2
    x1_parts, h_parts = [], []
    for r in (0, half):
        rows = slice(r, r + half)
        mixed = (_dot(fox_ref[rows, :], wo_ref[0:FOX_WIDTH, :])
                 + _dot(diff_ref[rows, :], wo_ref[FOX_WIDTH:, :]))
        x1_parts.append(x_ref[rows, :] + g1_ref[...] * mixed)
        h_parts.append(_rms_modulate(x1_parts[-1], ln_ref[...], sh_ref[...], sc_ref[...]).astype(BF16))
    x1 = jnp.concatenate(x1_parts, axis=0)
    h = jnp.concatenate(h_parts, axis=0)
    d_ff = wg_ref.shape[1]
    tiles = d_ff // MXU_WIDTH
    bounds = [(tiles * c // FFN_CHUNKS) * MXU_WIDTH for c in range(FFN_CHUNKS)] + [d_ff]
    o_ref[...] = x1
    for c in range(FFN_CHUNKS):
        cols = slice(bounds[c], bounds[c + 1])
        gate = _dot(h, wg_ref[:, cols])
        up = _dot(h, wu_ref[:, cols])
        a = ((gate / (1.0 + jnp.exp(-gate))) * up).astype(BF16)
        o_ref[...] += g2_ref[...] * _dot(a, wd_ref[cols, :])


def _mlp(layer, fox, diff, x, w_o, mod, ln_g, w_gate, w_up, w_down):
    batch, seq, d = x.shape
    tm = TOKEN_TILE
    tok = lambda width: pl.BlockSpec((None, tm, width), lambda b, s: (b, s, 0))
    mod_chunk = lambda i: pl.BlockSpec((None, None, None, 1, d), lambda b, s: (layer, b, i, 0, 0))
    stacked = lambda shape: pl.BlockSpec((None,) + tuple(shape[1:]), lambda b, s: (layer, 0, 0),
                                         pipeline_mode=pl.Buffered(1))
    return pl.pallas_call(
        _mlp_kernel,
        grid=(batch, seq // tm),
        in_specs=[
            tok(FOX_WIDTH), tok(DIFF_WIDTH), tok(d), stacked(w_o.shape), mod_chunk(2), stacked(ln_g.shape),
            mod_chunk(3), mod_chunk(4), mod_chunk(5),
            stacked(w_gate.shape), stacked(w_up.shape), stacked(w_down.shape),
        ],
        out_specs=tok(d),
        out_shape=jax.ShapeDtypeStruct((batch, seq, d), F32),
        compiler_params=pltpu.CompilerParams(
            dimension_semantics=("arbitrary", "arbitrary"), vmem_limit_bytes=VMEM_LIMIT),
        name="out_mlp",
    )(fox, diff, x, w_o, mod, ln_g, mod, mod, mod, w_gate, w_up, w_down)


def kernel(x, c, ln1_g, ln2_g, w_ada, b_ada, w_in, b_f, fox_qk_g, diff_qk_g, diff_lam, diff_norm_g,
           w_out, w_gate, w_up, w_down):
    depth = w_in.shape[0]
    batch, _, d = x.shape
    mod = _ada(c, w_ada, b_ada).reshape(depth, batch, MOD_CHUNKS, 1, d)
    fox_cols = 3 * FOX_WIDTH
    w_f = w_in[:, :, :fox_cols].astype(BF16)
    w_d = w_in[:, :, fox_cols + FOX_HEADS:].astype(BF16)
    w_o, w_g, w_u, w_dn = (w.astype(BF16) for w in (w_out, w_gate, w_up, w_down))
    w_gt = jnp.zeros((depth, BF16_SUBLANES, d), BF16).at[:, :FOX_HEADS].set(
        jnp.transpose(w_in[:, :, fox_cols:fox_cols + FOX_HEADS], (0, 2, 1)).astype(BF16))
    b_fp = jnp.zeros((depth, BF16_SUBLANES, 1), F32).at[:, :FOX_HEADS, 0].set(b_f)
    ln1 = ln1_g.reshape(depth, 1, d)
    ln2 = ln2_g.reshape(depth, 1, d)
    for l in range(depth):
        per_head = lambda g, n: jnp.tile(g, n).reshape(1, n * HEAD_DIM)
        fq, fk, fv, kb, dq, dk, dv = _inproj(
            l, x, ln1, mod, w_f, w_gt, w_d, b_fp,
            per_head(fox_qk_g[l, 0], FOX_HEADS), per_head(fox_qk_g[l, 1], FOX_HEADS),
            per_head(diff_qk_g[l, 0], 2 * DIFF_HEADS), per_head(diff_qk_g[l, 1], 2 * DIFF_HEADS))
        lam_init = 0.8 - 0.6 * math.exp(-0.3 * l)
        fox, diff = _attention(fq, fk, fv, kb, dq, dk, dv, diff_lam[l],
                               diff_norm_g[l].reshape(1, 2 * HEAD_DIM), lam_init)
        x = _mlp(l, fox, diff, x, w_o, mod, ln2, w_g, w_u, w_dn)
    return x
```

```python
import functools
import math

import jax
import jax.numpy as jnp
from jax import lax
from jax.experimental import pallas as pl
from jax.experimental.pallas import tpu as pltpu

D_MODEL = 1024
HEAD_DIM = 64
FOX_HEADS = 8
DIFF_HEADS = 4
FOX_WIDTH = FOX_HEADS * HEAD_DIM
DIFF_WIDTH = DIFF_HEADS * 2 * HEAD_DIM
MOD_CHUNKS = 6
EPS = 1e-6

LANES = 128
MXU_WIDTH = 256
BF16_SUBLANES = 16
LOG2E = math.log2(math.e)
QK_SCALE = LOG2E / math.sqrt(HEAD_DIM)
NEG = -1e30

TOKEN_TILE = 1024
CUMSUM_CHUNK = MXU_WIDTH
GROUPS_PER_STEP = 2
ATTN_TILE = 512
FFN_CHUNKS = 4
VMEM_LIMIT = 56 * 1024 * 1024

BF16 = jnp.bfloat16
F32 = jnp.float32

_NT = (((1,), (1,)), ((), ()))


def _dot(a, b):
    return jnp.dot(a, b, preferred_element_type=F32)


def _rms_modulate(x, gain, shift, scale):
    ms = jnp.mean(x * x, axis=-1, keepdims=True)
    return (x * lax.rsqrt(ms + EPS)) * (gain * (1.0 + scale)) + shift


def _ada_kernel(c_ref, w_ref, b_ref, o_ref):
    c = c_ref[...]
    cond = c / (1.0 + jnp.exp(-c))
    o_ref[...] = _dot(cond.astype(BF16), w_ref[...].astype(BF16)) + b_ref[...]


def _ada(c, w_ada, b_ada):
    depth, d, n = w_ada.shape
    batch = c.shape[0]
    nb = n // d
    return pl.pallas_call(
        _ada_kernel,
        grid=(depth, nb),
        in_specs=[
            pl.BlockSpec((batch, d), lambda l, j: (0, 0)),
            pl.BlockSpec((None, d, d), lambda l, j: (l, 0, j)),
            pl.BlockSpec((None, 1, d), lambda l, j: (l, 0, j)),
        ],
        out_specs=pl.BlockSpec((None, batch, d), lambda l, j: (l, 0, j)),
        out_shape=jax.ShapeDtypeStruct((depth, batch, n), F32),
        name="ada_mod",
    )(c, w_ada, b_ada.reshape(depth, 1, n))


def _head_rms(u, gain):
    rows, width = u.shape
    lane = lax.broadcasted_iota(jnp.int32, (rows, LANES), 1)
    lo = lane < HEAD_DIM
    outs = []
    for g in range(width // LANES):
        blk = u[:, g * LANES:(g + 1) * LANES]
        sq = blk * blk
        ss_lo = jnp.sum(jnp.where(lo, sq, 0.0), axis=-1, keepdims=True)
        ss_hi = jnp.sum(jnp.where(lo, 0.0, sq), axis=-1, keepdims=True)
        r = jnp.where(lo, lax.rsqrt(ss_lo / HEAD_DIM + EPS), lax.rsqrt(ss_hi / HEAD_DIM + EPS))
        outs.append(blk * r * gain[:, g * LANES:(g + 1) * LANES])
    return jnp.concatenate(outs, axis=-1)


def _inproj_kernel(x_ref, g_ref, sh_ref, sc_ref, wf_ref, wg_ref, wd_ref, bf_ref,
                   fqg_ref, fkg_ref, dqg_ref, dkg_ref,
                   fq_ref, fk_ref, fv_ref, kb_ref, dq_ref, dk_ref, dv_ref, carry_ref):
    tm = x_ref.shape[0]

    @pl.when(pl.program_id(1) == 0)
    def _():
        carry_ref[...] = jnp.zeros_like(carry_ref)

    h = _rms_modulate(x_ref[...], g_ref[...], sh_ref[...], sc_ref[...]).astype(BF16)

    w = FOX_WIDTH
    fq_ref[...] = _head_rms(_dot(h, wf_ref[:, 0:w]), fqg_ref[...] * QK_SCALE).astype(BF16)
    fk_ref[...] = _head_rms(_dot(h, wf_ref[:, w:2 * w]), fkg_ref[...]).astype(BF16)
    fv_ref[...] = _dot(h, wf_ref[:, 2 * w:3 * w]).astype(BF16)
    w = DIFF_WIDTH
    dq_ref[...] = _head_rms(_dot(h, wd_ref[:, 0:w]), dqg_ref[...] * QK_SCALE).astype(BF16)
    dk_ref[...] = _head_rms(_dot(h, wd_ref[:, w:2 * w]), dkg_ref[...]).astype(BF16)
    dv_ref[...] = _dot(h, wd_ref[:, 2 * w:3 * w]).astype(BF16)

    z = lax.dot_general(wg_ref[...], h, _NT, preferred_element_type=F32) + bf_ref[...]
    logf = jnp.minimum(z, 0.0) - jnp.log1p(jnp.exp(-jnp.abs(z)))

    cw = CUMSUM_CHUNK
    src = lax.broadcasted_iota(jnp.int32, (cw, cw), 0)
    dst = lax.broadcasted_iota(jnp.int32, (cw, cw), 1)
    tri = jnp.where(src <= dst, 1.0, 0.0).astype(BF16)
    segs = jnp.concatenate([logf[:, c0:c0 + cw] for c0 in range(0, tm, cw)], axis=0)
    hi = segs.astype(BF16)
    r1 = segs - hi.astype(F32)
    mid = r1.astype(BF16)
    low = (r1 - mid.astype(F32)).astype(BF16)
    parts = _dot(jnp.concatenate([hi, mid, low], axis=0), tri)
    n = segs.shape[0]
    local = parts[0:n] + parts[n:2 * n] + parts[2 * n:3 * n]
    totals = jnp.sum(segs, axis=-1, keepdims=True)
    carry = carry_ref[:, 0:1]
    rows = logf.shape[0]
    for c in range(tm // cw):
        blk = slice(c * rows, (c + 1) * rows)
        cum = carry + local[blk]
        kb_ref[:, c * cw:(c + 1) * cw] = (-LOG2E) * cum[0:FOX_HEADS, :]
        carry = carry + totals[blk]
    carry_ref[...] = jnp.broadcast_to(carry, carry_ref.shape)


def _inproj(layer, x, gain, mod, w_f, w_gt, w_d, b_f, fqg, fkg, dqg, dkg):
    batch, seq, d = x.shape
    tm = TOKEN_TILE
    tok = lambda width: pl.BlockSpec((None, tm, width), lambda b, s: (b, s, 0))
    mod_chunk = lambda i: pl.BlockSpec((None, None, None, 1, d), lambda b, s: (layer, b, i, 0, 0))
    const = lambda shape: pl.BlockSpec(shape, lambda b, s: (0,) * len(shape),
                                       pipeline_mode=pl.Buffered(1))
    stacked = lambda shape: pl.BlockSpec((None,) + tuple(shape[1:]), lambda b, s: (layer, 0, 0),
                                         pipeline_mode=pl.Buffered(1))
    act = lambda width: jax.ShapeDtypeStruct((batch, seq, width), BF16)
    return pl.pallas_call(
        _inproj_kernel,
        grid=(batch, seq // tm),
        in_specs=[
            tok(d), stacked(gain.shape), mod_chunk(0), mod_chunk(1),
            stacked(w_f.shape), stacked(w_gt.shape), stacked(w_d.shape), stacked(b_f.shape),
            const((1, FOX_WIDTH)), const((1, FOX_WIDTH)), const((1, DIFF_WIDTH)), const((1, DIFF_WIDTH)),
        ],
        out_specs=[
            tok(FOX_WIDTH), tok(FOX_WIDTH), tok(FOX_WIDTH),
            pl.BlockSpec((None, FOX_HEADS, tm), lambda b, s: (b, 0, s)),
            tok(DIFF_WIDTH), tok(DIFF_WIDTH), tok(DIFF_WIDTH),
        ],
        out_shape=[
            act(FOX_WIDTH), act(FOX_WIDTH), act(FOX_WIDTH),
            jax.ShapeDtypeStruct((batch, FOX_HEADS, seq), F32),
            act(DIFF_WIDTH), act(DIFF_WIDTH), act(DIFF_WIDTH),
        ],
        scratch_shapes=[pltpu.VMEM((BF16_SUBLANES, LANES), F32)],
        compiler_params=pltpu.CompilerParams(
            dimension_semantics=("arbitrary", "arbitrary"), vmem_limit_bytes=VMEM_LIMIT),
        name="inproj",
    )(x, gain, mod, mod, w_f, w_gt, w_d, b_f, fqg, fkg, dqg, dkg)


def _causal(s):
    row = lax.broadcasted_iota(jnp.int32, s.shape, 0)
    col = lax.broadcasted_iota(jnp.int32, s.shape, 1)
    return jnp.where(col <= row, s, NEG)


def _split_halves(q):
    lane = lax.broadcasted_iota(jnp.int32, q.shape, 1)
    lo = lane < HEAD_DIM
    zero = jnp.zeros_like(q)
    return lo, jnp.where(lo, q, zero), jnp.where(lo, zero, q)


def _flash(qs, k_ref, v_ref, bias_fn, qi, t):
    row0 = qi * t
    half = t // 2
    steps = [(0, key0, t, 0) for key0 in range(0, row0, t)]
    steps += [(0, row0, half, half), (half, row0 + half, half, half)]
    m2 = [jnp.full((t, 1), NEG, F32)] * 2
    acc2 = [None, None]
    for r0, key0, width, tri in steps:
        keys = slice(key0, key0 + width)
        k = k_ref[keys, :]
        v = v_ref[keys, :]
        v_ones = jnp.concatenate([v, jnp.ones_like(v)], axis=1)
        rows = t - r0
        s2 = lax.dot_general(jnp.concatenate([q[r0:] for q in qs], axis=0), k, _NT,
                             preferred_element_type=F32)
        ps, alphas = [], []
        for n, bias in enumerate(bias_fn(key0, width)):
            s = s2[n * rows:(n + 1) * rows] + bias
            if tri:
                s = jnp.concatenate([_causal(s[:tri]), s[tri:]], axis=0) if tri < s.shape[0] else _causal(s)
            m_old = m2[n][r0:]
            m_new = jnp.maximum(m_old, jnp.max(s, axis=-1, keepdims=True))
            ps.append(jnp.exp2(s - m_new).astype(BF16))
            alphas.append(jnp.exp2(m_old - m_new))
            m2[n] = m_new if r0 == 0 else jnp.concatenate([m2[n][:r0], m_new], axis=0)
        pv2 = _dot(jnp.concatenate(ps, axis=0), v_ones)
        for n in range(2):
            pv = pv2[n * rows:(n + 1) * rows]
            acc = pv if acc2[n] is None else acc2[n][r0:] * alphas[n] + pv
            acc2[n] = acc if r0 == 0 else jnp.concatenate([acc2[n][:r0], acc], axis=0)
    return acc2


def _fox_kernel(group, q_ref, k_ref, v_ref, kb_ref, o_ref):
    t = ATTN_TILE

    def bias_fn(start, width):
        head = 2 * group
        return tuple(kb_ref[pl.ds(head + n, 1), start:start + width] for n in range(2))

    for qi in range(q_ref.shape[0] // t):
        rows = slice(qi * t, (qi + 1) * t)
        lo, q0, q1 = _split_halves(q_ref[rows, :])
        acc0, acc1 = _flash((q0, q1), k_ref, v_ref, bias_fn, qi, t)
        out = jnp.where(lo, acc0[:, :LANES], acc1[:, :LANES])
        sums = jnp.where(lo, acc0[:, LANES:], acc1[:, LANES:])
        o_ref[rows, :] = (out / sums).astype(o_ref.dtype)


def _diff_kernel(head, q_ref, k_ref, v_ref, lam_ref, ng_ref, o_ref, *, lam_init):
    t = ATTN_TILE
    slope = jnp.exp2(-2.0 * (jnp.full((1, 1), head, jnp.int32) + 1).astype(F32))
    lv = lam_ref[...]
    lam = (jnp.exp(jnp.sum(lv[0:1] * lv[1:2], axis=-1, keepdims=True))
           - jnp.exp(jnp.sum(lv[2:3] * lv[3:4], axis=-1, keepdims=True)) + lam_init)

    def bias_fn(start, width):
        pos = (start + lax.broadcasted_iota(jnp.int32, (1, width), 1)).astype(F32)
        kb = (LOG2E * slope) * pos
        return kb, kb

    for qi in range(q_ref.shape[0] // t):
        rows = slice(qi * t, (qi + 1) * t)
        _, q1, q2 = _split_halves(q_ref[rows, :])
        acc1, acc2 = _flash((q1, q2), k_ref, v_ref, bias_fn, qi, t)
        o = acc1[:, :LANES] / acc1[:, LANES:] - lam * (acc2[:, :LANES] / acc2[:, LANES:])
        ms = jnp.mean(o * o, axis=-1, keepdims=True)
        o_ref[rows, :] = ((o * lax.rsqrt(ms + EPS) * ng_ref[...]) * (1.0 - lam_init)).astype(o_ref.dtype)


def _attn_kernel(fq_ref, fk_ref, fv_ref, kb_ref, dq_ref, dk_ref, dv_ref, lam_ref, ng_ref,
                 fox_ref, diff_ref, *, lam_init):
    for g in range(GROUPS_PER_STEP):
        group = GROUPS_PER_STEP * pl.program_id(1) + g
        col = lambda ref: ref.at[:, g * LANES:(g + 1) * LANES]
        _fox_kernel(group, col(fq_ref), col(fk_ref), col(fv_ref), kb_ref, col(fox_ref))
        _diff_kernel(group, col(dq_ref), col(dk_ref), col(dv_ref), lam_ref, ng_ref, col(diff_ref),
                     lam_init=lam_init)


def _attention(fq, fk, fv, kb, dq, dk, dv, lam_params, norm_gain, lam_init):
    batch, seq, width = fq.shape
    groups = width // LANES
    assert dq.shape[2] // LANES == groups
    seq_block = pl.BlockSpec((None, seq, GROUPS_PER_STEP * LANES), lambda b, h: (b, 0, h))
    out = jax.ShapeDtypeStruct((batch, seq, width), BF16)
    return pl.pallas_call(
        functools.partial(_attn_kernel, lam_init=lam_init),
        grid=(batch, groups // GROUPS_PER_STEP),
        in_specs=[seq_block, seq_block, seq_block,
                  pl.BlockSpec((None,) + kb.shape[1:], lambda b, h: (b, 0, 0)),
                  seq_block, seq_block, seq_block,
                  pl.BlockSpec(lam_params.shape, lambda b, h: (0, 0)),
                  pl.BlockSpec(norm_gain.shape, lambda b, h: (0, 0))],
        out_specs=[seq_block, seq_block],
        out_shape=[out, out],
        compiler_params=pltpu.CompilerParams(dimension_semantics=("arbitrary", "arbitrary")),
        name="attention",
    )(fq, fk, fv, kb, dq, dk, dv, lam_params, norm_gain)


def _mlp_kernel(fox_ref, diff_ref, x_ref, wo_ref, g1_ref, ln_ref, sh_ref, sc_ref, g2_ref,
                wg_ref, wu_ref, wd_ref, o_ref):
    half = x_ref.shape[0] // 2
    x1_parts, h_parts = [], []
    for r in (0, half):
        rows = slice(r, r + half)
        mixed = (_dot(fox_ref[rows, :], wo_ref[0:FOX_WIDTH, :])
                 + _dot(diff_ref[rows, :], wo_ref[FOX_WIDTH:, :]))
        x1_parts.append(x_ref[rows, :] + g1_ref[...] * mixed)
        h_parts.append(_rms_modulate(x1_parts[-1], ln_ref[...], sh_ref[...], sc_ref[...]).astype(BF16))
    x1 = jnp.concatenate(x1_parts, axis=0)
    h = jnp.concatenate(h_parts, axis=0)
    d_ff = wg_ref.shape[1]
    tiles = d_ff // MXU_WIDTH
    bounds = [(tiles * c // FFN_CHUNKS) * MXU_WIDTH for c in range(FFN_CHUNKS)] + [d_ff]
    o_ref[...] = x1
    for c in range(FFN_CHUNKS):
        cols = slice(bounds[c], bounds[c + 1])
        gate = _dot(h, wg_ref[:, cols])
        up = _dot(h, wu_ref[:, cols])
        a = ((gate / (1.0 + jnp.exp(-gate))) * up).astype(BF16)
        o_ref[...] += g2_ref[...] * _dot(a, wd_ref[cols, :])


def _mlp(layer, fox, diff, x, w_o, mod, ln_g, w_gate, w_up, w_down):
    batch, seq, d = x.shape
    tm = TOKEN_TILE
    tok = lambda width: pl.BlockSpec((None, tm, width), lambda b, s: (b, s, 0))
    mod_chunk = lambda i: pl.BlockSpec((None, None, None, 1, d), lambda b, s: (layer, b, i, 0, 0))
    stacked = lambda shape: pl.BlockSpec((None,) + tuple(shape[1:]), lambda b, s: (layer, 0, 0),
                                         pipeline_mode=pl.Buffered(1))
    return pl.pallas_call(
        _mlp_kernel,
        grid=(batch, seq // tm),
        in_specs=[
            tok(FOX_WIDTH), tok(DIFF_WIDTH), tok(d), stacked(w_o.shape), mod_chunk(2), stacked(ln_g.shape),
            mod_chunk(3), mod_chunk(4), mod_chunk(5),
            stacked(w_gate.shape), stacked(w_up.shape), stacked(w_down.shape),
        ],
        out_specs=tok(d),
        out_shape=jax.ShapeDtypeStruct((batch, seq, d), F32),
        compiler_params=pltpu.CompilerParams(
            dimension_semantics=("arbitrary", "arbitrary"), vmem_limit_bytes=VMEM_LIMIT),
        name="out_mlp",
    )(fox, diff, x, w_o, mod, ln_g, mod, mod, mod, w_gate, w_up, w_down)


def kernel(x, c, ln1_g, ln2_g, w_ada, b_ada, w_in, b_f, fox_qk_g, diff_qk_g, diff_lam, diff_norm_g,
           w_out, w_gate, w_up, w_down):
    depth = w_in.shape[0]
    batch, _, d = x.shape
    mod = _ada(c, w_ada, b_ada).reshape(depth, batch, MOD_CHUNKS, 1, d)
    fox_cols = 3 * FOX_WIDTH
    w_f = w_in[:, :, :fox_cols].astype(BF16)
    w_d = w_in[:, :, fox_cols + FOX_HEADS:].astype(BF16)
    w_o, w_g, w_u, w_dn = (w.astype(BF16) for w in (w_out, w_gate, w_up, w_down))
    w_gt = jnp.zeros((depth, BF16_SUBLANES, d), BF16).at[:, :FOX_HEADS].set(
        jnp.transpose(w_in[:, :, fox_cols:fox_cols + FOX_HEADS], (0, 2, 1)).astype(BF16))
    b_fp = jnp.zeros((depth, BF16_SUBLANES, 1), F32).at[:, :FOX_HEADS, 0].set(b_f)
    ln1 = ln1_g.reshape(depth, 1, d)
    ln2 = ln2_g.reshape(depth, 1, d)
    for l in range(depth):
        per_head = lambda g, n: jnp.tile(g, n).reshape(1, n * HEAD_DIM)
        fq, fk, fv, kb, dq, dk, dv = _inproj(
            l, x, ln1, mod, w_f, w_gt, w_d, b_fp,
            per_head(fox_qk_g[l, 0], FOX_HEADS), per_head(fox_qk_g[l, 1], FOX_HEADS),
            per_head(diff_qk_g[l, 0], 2 * DIFF_HEADS), per_head(diff_qk_g[l, 1], 2 * DIFF_HEADS))
        lam_init = 0.8 - 0.6 * math.exp(-0.3 * l)
        fox, diff = _attention(fq, fk, fv, kb, dq, dk, dv, diff_lam[l],
                               diff_norm_g[l].reshape(1, 2 * HEAD_DIM), lam_init)
        x = _mlp(l, fox, diff, x, w_o, mod, ln2, w_g, w_u, w_dn)
    return x
```
